```python
import jax, jax.numpy as jnp
from jax import lax
import numpy as np

D_MODEL = 2048
BATCH = 8
SEQ = 2048
DEPTH = 2

N_MEM = 256
CONV_CH = 1024
CONV_WIDTH = 31
HEAD_DIM = 128
N_Q_HEADS = 8
N_KV_HEADS = 2
Q_WIDTH = N_Q_HEADS * HEAD_DIM
KV_WIDTH = N_KV_HEADS * HEAD_DIM
WINDOW = 128
BLOCK = 128
ROT_DIM = HEAD_DIM // 4
ROPE_THETA = 500000.0
MEM_HEADS = 4
MEM_HEAD_DIM = 256
MEM_WIDTH = MEM_HEADS * MEM_HEAD_DIM
N_BRANCH = 3
D_FF = 5632
IN_SIZES = (2 * CONV_CH, Q_WIDTH, KV_WIDTH, KV_WIDTH, MEM_WIDTH, N_BRANCH * D_MODEL)
IN_WIDTH = 2 * CONV_CH + Q_WIDTH + 2 * KV_WIDTH + MEM_WIDTH + N_BRANCH * D_MODEL
ALPHA = (2 * DEPTH) ** 0.25
BETA = (8 * DEPTH) ** -0.25
LN_EPS = 1e-5
NEG_INF = -1e30

kernel_name = "hybrid_conv_swa_memory_macaron_deepnorm"


def layer_norm(x, g, b):
    xf = x.astype(jnp.float32)
    mu = jnp.mean(xf, axis=-1, keepdims=True)
    var = jnp.mean(jnp.square(xf - mu), axis=-1, keepdims=True)
    y = (xf - mu) * lax.rsqrt(var + LN_EPS) * g.astype(jnp.float32) + b.astype(jnp.float32)
    return y.astype(x.dtype)


def swiglu(x, w_up, w_down):
    gu = x @ w_up
    gate, up = jnp.split(gu, 2, axis=-1)
    return (jax.nn.silu(gate) * up) @ w_down


def rope_tables(seq_len):
    pos = jnp.arange(seq_len, dtype=jnp.float32)
    inv_freq = ROPE_THETA ** (-jnp.arange(0, ROT_DIM, 2, dtype=jnp.float32) / ROT_DIM)
    ang = pos[:, None] * inv_freq[None, :]
    return jnp.cos(ang), jnp.sin(ang)


def apply_partial_rope(x, cos, sin):
    c = cos[None, :, None, :].astype(x.dtype)
    s = sin[None, :, None, :].astype(x.dtype)
    x1 = x[..., : ROT_DIM // 2]
    x2 = x[..., ROT_DIM // 2: ROT_DIM]
    return jnp.concatenate([x1 * c - x2 * s, x2 * c + x1 * s, x[..., ROT_DIM:]], axis=-1)


def conv_module(u, dw_w, dw_b, ln_g, ln_b, w_pw):
    a, g = jnp.split(u, 2, axis=-1)
    h = a * jax.nn.sigmoid(g)
    h = lax.conv_general_dilated(
        h, dw_w[:, None, :].astype(h.dtype), window_strides=(1,),
        padding=[(CONV_WIDTH // 2, CONV_WIDTH // 2)],
        dimension_numbers=("NWC", "WIO", "NWC"),
        feature_group_count=CONV_CH) + dw_b
    h = jax.nn.silu(layer_norm(h, ln_g, ln_b))
    return h @ w_pw


def window_attention(q, k, v, sink, w_o):
    B, S = q.shape[0], q.shape[1]
    nb = S // BLOCK
    G = N_Q_HEADS // N_KV_HEADS
    qb = q.reshape(B, nb, BLOCK, N_KV_HEADS, G, HEAD_DIM)
    pad = ((0, 0), (BLOCK, BLOCK), (0, 0), (0, 0))
    kp = jnp.pad(k, pad).reshape(B, nb + 2, BLOCK, N_KV_HEADS, HEAD_DIM)
    vp = jnp.pad(v, pad).reshape(B, nb + 2, BLOCK, N_KV_HEADS, HEAD_DIM)
    kb = jnp.concatenate([kp[:, :-2], kp[:, 1:-1], kp[:, 2:]], axis=2)
    vb = jnp.concatenate([vp[:, :-2], vp[:, 1:-1], vp[:, 2:]], axis=2)
    scores = jnp.einsum("bnqkgd,bnckd->bnkgqc", qb, kb).astype(jnp.float32) * (HEAD_DIM ** -0.5)
    qpos = jnp.arange(BLOCK)[:, None] + BLOCK
    kpos = jnp.arange(3 * BLOCK)[None, :]
    kabs = jnp.arange(nb)[:, None, None] * BLOCK + kpos[None] - BLOCK
    valid = (jnp.abs(kpos - qpos) <= WINDOW)[None] & (kabs >= 0) & (kabs < S)
    scores = jnp.where(valid[None, :, None, None], scores, NEG_INF)
    sink_b = jnp.broadcast_to(sink.astype(jnp.float32).reshape(1, 1, N_KV_HEADS, G, 1, 1),
                              scores.shape[:-1] + (1,))
    p = jax.nn.softmax(jnp.concatenate([scores, sink_b], axis=-1), axis=-1)[..., :-1]
    out = jnp.einsum("bnkgqc,bnckd->bnqkgd", p.astype(vb.dtype), vb)
    return out.reshape(B, S, Q_WIDTH) @ w_o


def memory_attention(q, mk, mv, w_o):
    B, S = q.shape[0], q.shape[1]
    scores = jnp.einsum("bshd,bmhd->bhsm", q, mk).astype(jnp.float32) * (MEM_HEAD_DIM ** -0.5)
    p = jax.nn.softmax(scores, axis=-1)
    out = jnp.einsum("bhsm,bmhd->bshd", p.astype(mv.dtype), mv)
    return out.reshape(B, S, MEM_WIDTH) @ w_o


def token_mixer(h, mem, cos, sin, w_in, conv_dw_w, conv_dw_b, conv_ln_g, conv_ln_b, conv_w_out,
                win_w_o, win_sink, mem_w_kv, mem_w_o, w_out):
    B, S, _ = h.shape
    proj = h @ w_in
    offs = np.cumsum(IN_SIZES)[:-1].tolist()
    u_conv, q, k, v, q_mem, g_logits = jnp.split(proj, offs, axis=-1)
    y_conv = conv_module(u_conv, conv_dw_w, conv_dw_b, conv_ln_g, conv_ln_b, conv_w_out)
    q = apply_partial_rope(q.reshape(B, S, N_Q_HEADS, HEAD_DIM), cos, sin)
    k = apply_partial_rope(k.reshape(B, S, N_KV_HEADS, HEAD_DIM), cos, sin)
    v = v.reshape(B, S, N_KV_HEADS, HEAD_DIM)
    y_win = window_attention(q, k, v, win_sink, win_w_o)
    mk, mv = jnp.split(mem @ mem_w_kv, 2, axis=-1)
    M = mem.shape[1]
    y_mem = memory_attention(q_mem.reshape(B, S, MEM_HEADS, MEM_HEAD_DIM),
                             mk.reshape(B, M, MEM_HEADS, MEM_HEAD_DIM),
                             mv.reshape(B, M, MEM_HEADS, MEM_HEAD_DIM), mem_w_o)
    gates = jax.nn.sigmoid(g_logits.astype(jnp.float32)).astype(h.dtype).reshape(B, S, N_BRANCH, D_MODEL)
    merged = gates[:, :, 0] * y_conv + gates[:, :, 1] * y_win + gates[:, :, 2] * y_mem
    return merged @ w_out


def setup_inputs(seed: int = 0) -> dict:
    key = jax.random.key(seed)
    ks = jax.random.split(key, 24)

    def nrm(k, shape, scale):
        return jax.random.normal(k, shape, jnp.float32) * scale

    L, D = DEPTH, D_MODEL
    return {
        "x": nrm(ks[0], (BATCH, SEQ, D), 1.0),
        "mem": nrm(ks[1], (BATCH, N_MEM, D), 1.0),
        "ln1_g": 1.0 + nrm(ks[2], (L, D), 0.02),
        "ln1_b": nrm(ks[3], (L, D), 0.02),
        "ffn1_w_up": nrm(ks[4], (L, D, 2 * D_FF), D ** -0.5),
        "ffn1_w_down": nrm(ks[5], (L, D_FF, D), BETA * D_FF ** -0.5),
        "w_in": nrm(ks[6], (L, D, IN_WIDTH), D ** -0.5),
        "conv_dw_w": nrm(ks[7], (L, CONV_WIDTH, CONV_CH), CONV_WIDTH ** -0.5),
        "conv_dw_b": nrm(ks[8], (L, CONV_CH), 0.02),
        "conv_ln_g": 1.0 + nrm(ks[9], (L, CONV_CH), 0.02),
        "conv_ln_b": nrm(ks[10], (L, CONV_CH), 0.02),
        "conv_w_out": nrm(ks[11], (L, CONV_CH, D), BETA * CONV_CH ** -0.5),
        "win_w_o": nrm(ks[12], (L, Q_WIDTH, D), BETA * Q_WIDTH ** -0.5),
        "win_sink": nrm(ks[13], (L, N_Q_HEADS), 0.5),
        "mem_w_kv": nrm(ks[14], (L, D, 2 * MEM_WIDTH), D ** -0.5),
        "mem_w_o": nrm(ks[15], (L, MEM_WIDTH, D), BETA * MEM_WIDTH ** -0.5),
        "w_out": nrm(ks[16], (L, D, D), BETA * D ** -0.5),
        "ln2_g": 1.0 + nrm(ks[17], (L, D), 0.02),
        "ln2_b": nrm(ks[18], (L, D), 0.02),
        "ffn2_w_up": nrm(ks[19], (L, D, 2 * D_FF), D ** -0.5),
        "ffn2_w_down": nrm(ks[20], (L, D_FF, D), BETA * D_FF ** -0.5),
        "ln3_g": 1.0 + nrm(ks[21], (L, D), 0.02),
        "ln3_b": nrm(ks[22], (L, D), 0.02),
    }


def reference(x, mem, ln1_g, ln1_b, ffn1_w_up, ffn1_w_down, w_in, conv_dw_w, conv_dw_b,
              conv_ln_g, conv_ln_b, conv_w_out, win_w_o, win_sink, mem_w_kv, mem_w_o, w_out,
              ln2_g, ln2_b, ffn2_w_up, ffn2_w_down, ln3_g, ln3_b):
    cos, sin = rope_tables(x.shape[1])
    h = x
    for l in range(DEPTH):
        h = layer_norm(ALPHA * h + 0.5 * swiglu(h, ffn1_w_up[l], ffn1_w_down[l]), ln1_g[l], ln1_b[l])
        y = token_mixer(h, mem, cos, sin, w_in[l], conv_dw_w[l], conv_dw_b[l], conv_ln_g[l],
                        conv_ln_b[l], conv_w_out[l], win_w_o[l], win_sink[l], mem_w_kv[l],
                        mem_w_o[l], w_out[l])
        h = layer_norm(ALPHA * h + y, ln2_g[l], ln2_b[l])
        h = layer_norm(ALPHA * h + 0.5 * swiglu(h, ffn2_w_up[l], ffn2_w_down[l]), ln3_g[l], ln3_b[l])
    return h
```

```python
import functools

import jax
import jax.numpy as jnp
from jax import lax
from jax.experimental import pallas as pl
from jax.experimental.pallas import tpu as pltpu

D_MODEL = 2048
DEPTH = 2
N_MEM = 256
CONV_CH = 1024
CONV_WIDTH = 31
CONV_HALF = CONV_WIDTH // 2
HEAD_DIM = 128
N_Q_HEADS = 8
N_KV_HEADS = 2
GROUP = N_Q_HEADS // N_KV_HEADS
Q_WIDTH = N_Q_HEADS * HEAD_DIM
KV_WIDTH = N_KV_HEADS * HEAD_DIM
WINDOW = 128
BLOCK = 128
ROT_DIM = HEAD_DIM // 4
ROT_HALF = ROT_DIM // 2
ROPE_THETA = 500000.0
MEM_HEADS = 4
MEM_HEAD_DIM = 256
MEM_WIDTH = MEM_HEADS * MEM_HEAD_DIM
N_BRANCH = 3
D_FF = 5632
ALPHA = (2 * DEPTH) ** 0.25
LN_EPS = 1e-5
NEG_INF = -1e30

OFF_CONV = 0
OFF_QKV = 2 * CONV_CH
QKVM_WIDTH = Q_WIDTH + 2 * KV_WIDTH + MEM_WIDTH
OFF_GATE = OFF_QKV + QKVM_WIDTH

V7X_LANES = 128
V7X_SUBLANES = 8
V7X_VMEM_BYTES = 64 * 1024 * 1024
VMEM_REQUEST_CAP = V7X_VMEM_BYTES - 8 * 1024 * 1024

F32 = jnp.float32
BF16 = jnp.bfloat16


def _vmem_limit(block_bytes, temp_bytes):
    need = int((2 * block_bytes + temp_bytes) * 1.25)
    return min(max(need, 16 * 1024 * 1024), VMEM_REQUEST_CAP)


def _params(semantics, vmem_bytes):
    return pltpu.CompilerParams(dimension_semantics=semantics, vmem_limit_bytes=vmem_bytes)


def _layer_norm_rows(z, g, b):
    mu = jnp.mean(z, axis=-1, keepdims=True)
    zc = z - mu
    var = jnp.mean(zc * zc, axis=-1, keepdims=True)
    return zc * lax.rsqrt(var + LN_EPS) * g + b


def _dot(a, b):
    return jnp.dot(a, b, preferred_element_type=F32)


def _dot_nt(a, b):
    return lax.dot_general(a, b, (((1,), (1,)), ((), ())), preferred_element_type=F32)


FFN_TM = 512
FFN_TF = 512


def _ffn_kernel(x_ref, wg_ref, wu_ref, wd_ref, g_ref, b_ref, o_ref, xb_ref):
    j = pl.program_id(1)

    @pl.when(j == 0)
    def _():
        xb_ref[...] = x_ref[...].astype(BF16)
        o_ref[...] = jnp.zeros_like(o_ref)

    xb = xb_ref[...]
    gate = _dot(xb, wg_ref[...])
    up = _dot(xb, wu_ref[...])
    h = (gate * jax.nn.sigmoid(gate) * up).astype(BF16)
    o_ref[...] += _dot(h, wd_ref[...])

    @pl.when(j == pl.num_programs(1) - 1)
    def _():
        z = ALPHA * x_ref[...] + 0.5 * o_ref[...]
        o_ref[...] = _layer_norm_rows(z, g_ref[...], b_ref[...])


def _ffn(x, w_up, w_down, ln_g, ln_b):
    t, d = x.shape
    nf = D_FF // FFN_TF
    blocks = (FFN_TM * d * 4 * 2 + 3 * d * FFN_TF * 2)
    temps = FFN_TM * d * 2 + 4 * FFN_TM * FFN_TF * 4 + FFN_TM * d * 4
    return pl.pallas_call(
        _ffn_kernel,
        grid=(t // FFN_TM, nf),
        in_specs=[
            pl.BlockSpec((FFN_TM, d), lambda i, j: (i, 0)),
            pl.BlockSpec((d, FFN_TF), lambda i, j: (0, j)),
            pl.BlockSpec((d, FFN_TF), lambda i, j: (0, nf + j)),
            pl.BlockSpec((FFN_TF, d), lambda i, j: (j, 0)),
            pl.BlockSpec((1, d), lambda i, j: (0, 0)),
            pl.BlockSpec((1, d), lambda i, j: (0, 0)),
        ],
        out_specs=pl.BlockSpec((FFN_TM, d), lambda i, j: (i, 0)),
        out_shape=jax.ShapeDtypeStruct((t, d), F32),
        scratch_shapes=[pltpu.VMEM((FFN_TM, d), BF16)],
        compiler_params=_params(("parallel", "arbitrary"), _vmem_limit(blocks, temps)),
        name="ffn",
    )(x, w_up, w_up, w_down, ln_g, ln_b)


PROJ_TM = 1024
PROJ_TN = 512


def _cast_rows_once(x_ref, xb_ref):
    @pl.when(pl.program_id(1) == 0)
    def _():
        xb_ref[...] = x_ref[...].astype(BF16)


def _glu_kernel(x_ref, wa_ref, wg_ref, o_ref, xb_ref):
    _cast_rows_once(x_ref, xb_ref)
    xb = xb_ref[...]
    a = _dot(xb, wa_ref[...])
    g = _dot(xb, wg_ref[...])
    o_ref[...] = a * jax.nn.sigmoid(g)


def _rope(r, cos, sin_lo, sin_hi):
    n = r.shape[1] // HEAD_DIM
    tile = lambda tbl: jnp.concatenate([tbl] * n, axis=1) if n > 1 else tbl
    from_lower = pltpu.roll(r, ROT_HALF, axis=1)
    from_upper = pltpu.roll(r, r.shape[1] - ROT_HALF, axis=1)
    return r * tile(cos) + from_lower * tile(sin_lo) + from_upper * tile(sin_hi)


KV_TILE = Q_WIDTH // PROJ_TN


def _qkvm_kernel(x_ref, w_ref, cos_ref, slo_ref, shi_ref, o_ref, xb_ref):
    _cast_rows_once(x_ref, xb_ref)
    j = pl.program_id(1)
    r = _dot(xb_ref[...], w_ref[...])

    @pl.when(j < KV_TILE)
    def _():
        o_ref[...] = _rope(r, cos_ref[...], slo_ref[...], shi_ref[...]).astype(o_ref.dtype)

    @pl.when(j == KV_TILE)
    def _():
        k = _rope(r[:, :KV_WIDTH], cos_ref[...], slo_ref[...], shi_ref[...])
        o_ref[...] = jnp.concatenate([k, r[:, KV_WIDTH:]], axis=1).astype(o_ref.dtype)

    @pl.when(j > KV_TILE)
    def _():
        o_ref[...] = r.astype(o_ref.dtype)


def _gate_kernel(x_ref, w_ref, o_ref, xb_ref):
    _cast_rows_once(x_ref, xb_ref)
    o_ref[...] = jax.nn.sigmoid(_dot(xb_ref[...], w_ref[...])).astype(o_ref.dtype)


def _proj_call(kernel, x, weights, w_col_blocks, extra, extra_specs, n_out, out_dtype, name):
    t, d = x.shape
    out_bytes = jnp.dtype(out_dtype).itemsize
    blocks = PROJ_TM * d * 4 + len(weights) * d * PROJ_TN * 2 + PROJ_TM * PROJ_TN * out_bytes
    temps = PROJ_TM * d * 2 + 6 * PROJ_TM * PROJ_TN * 4
    w_specs = [pl.BlockSpec((d, PROJ_TN), functools.partial(lambda i, j, off: (0, off + j), off=off))
               for off in w_col_blocks]
    return pl.pallas_call(
        kernel,
        grid=(t // PROJ_TM, n_out // PROJ_TN),
        in_specs=[pl.BlockSpec((PROJ_TM, d), lambda i, j: (i, 0))] + w_specs + extra_specs,
        out_specs=pl.BlockSpec((PROJ_TM, PROJ_TN), lambda i, j: (i, j)),
        out_shape=jax.ShapeDtypeStruct((t, n_out), out_dtype),
        scratch_shapes=[pltpu.VMEM((PROJ_TM, d), BF16)],
        compiler_params=_params(("parallel", "arbitrary"), _vmem_limit(blocks, temps)),
        name=name,
    )(x, *weights, *extra)


def _rope_tables(seq_len):
    pos = jnp.arange(seq_len, dtype=F32)
    inv_freq = ROPE_THETA ** (-jnp.arange(0, ROT_DIM, 2, dtype=F32) / ROT_DIM)
    ang = pos[:, None] * inv_freq[None, :]
    cos, sin = jnp.cos(ang), jnp.sin(ang)
    rest = HEAD_DIM - ROT_DIM
    zeros = lambda n: jnp.zeros((seq_len, n), F32)
    cos_t = jnp.concatenate([cos, cos, jnp.ones((seq_len, rest), F32)], axis=1)
    sin_lo = jnp.concatenate([zeros(ROT_HALF), sin, zeros(rest)], axis=1)
    sin_hi = jnp.concatenate([-sin, zeros(ROT_HALF), zeros(rest)], axis=1)
    return cos_t, sin_lo, sin_hi


CONV_ROWS = 64
CONV_PAD = 16
CONV_SHIFT_ROWS = 208
CONV_TAP0 = CONV_PAD - CONV_HALF


def _conv_kernel(x_ref, w_ref, cb_ref, g_ref, b_ref, o_ref, sh_ref, cv_ref):
    s, c = x_ref.shape
    padded = s + 2 * CONV_PAD
    tail = sh_ref.shape[1] - (CONV_PAD + s)
    assert padded % CONV_SHIFT_ROWS == 0 and padded + V7X_SUBLANES <= sh_ref.shape[1]

    for ct in range(c // V7X_LANES):
        lanes = slice(ct * V7X_LANES, (ct + 1) * V7X_LANES)
        sh_ref[0, pl.ds(0, CONV_PAD), :] = jnp.zeros((CONV_PAD, V7X_LANES), F32)
        sh_ref[0, pl.ds(CONV_PAD + s, tail), :] = jnp.zeros((tail, V7X_LANES), F32)
        sh_ref[0, pl.ds(CONV_PAD, s), :] = x_ref[:, lanes]

        def shift(i, carry):
            r0 = pl.multiple_of(i * CONV_SHIFT_ROWS, V7X_SUBLANES)
            win = sh_ref[0, pl.ds(r0, CONV_SHIFT_ROWS + V7X_SUBLANES), :]
            for p in range(1, V7X_SUBLANES):
                sh_ref[p, pl.ds(r0, CONV_SHIFT_ROWS), :] = win[p:p + CONV_SHIFT_ROWS]
            return carry

        lax.fori_loop(0, padded // CONV_SHIFT_ROWS, shift, 0)

        def taps(i, carry):
            r0 = pl.multiple_of(i * CONV_ROWS, CONV_ROWS)
            acc = jnp.broadcast_to(cb_ref[:, lanes], (CONV_ROWS, V7X_LANES))
            for k in range(CONV_WIDTH):
                off = CONV_TAP0 + k
                rows = pl.ds(r0 + (off // V7X_SUBLANES) * V7X_SUBLANES, CONV_ROWS)
                acc = acc + sh_ref[off % V7X_SUBLANES, rows, :] * w_ref[k:k + 1, lanes]
            cv_ref[pl.ds(r0, CONV_ROWS), lanes] = acc
            return carry

        lax.fori_loop(0, s // CONV_ROWS, taps, 0)

    def norm(i, carry):
        r0 = pl.multiple_of(i * CONV_ROWS, CONV_ROWS)
        y = _layer_norm_rows(cv_ref[pl.ds(r0, CONV_ROWS), :], g_ref[...], b_ref[...])
        o_ref[pl.ds(r0, CONV_ROWS), :] = (y * jax.nn.sigmoid(y)).astype(o_ref.dtype)
        return carry

    lax.fori_loop(0, s // CONV_ROWS, norm, 0)


def _conv_branch(glu, dw_w, dw_b, ln_g, ln_b):
    b, s, c = glu.shape
    sh_rows = s + 2 * CONV_PAD + V7X_SUBLANES
    blocks = s * c * 4 + s * c * 2 + CONV_WIDTH * c * 4
    temps = V7X_SUBLANES * sh_rows * V7X_LANES * 4 + s * c * 4
    row = lambda: pl.BlockSpec((1, c), lambda i: (0, 0))
    return pl.pallas_call(
        _conv_kernel,
        grid=(b,),
        in_specs=[
            pl.BlockSpec((None, s, c), lambda i: (i, 0, 0)),
            pl.BlockSpec((CONV_WIDTH, c), lambda i: (0, 0)),
            row(), row(), row(),
        ],
        out_specs=pl.BlockSpec((None, s, c), lambda i: (i, 0, 0)),
        out_shape=jax.ShapeDtypeStruct((b, s, c), BF16),
        scratch_shapes=[pltpu.VMEM((V7X_SUBLANES, sh_rows, V7X_LANES), F32), pltpu.VMEM((s, c), F32)],
        compiler_params=_params(("parallel",), _vmem_limit(blocks, temps)),
        name="conv_branch",
    )(glu, dw_w, dw_b, ln_g, ln_b)


WIN_KEYS = 3 * BLOCK


def _win_attn_kernel(sink_ref, q_ref, kv_ref, o_ref):
    s = q_ref.shape[0]
    rows = GROUP * BLOCK
    key_minus_query = (lax.broadcasted_iota(jnp.int32, (rows, WIN_KEYS), 1)
                       - lax.broadcasted_iota(jnp.int32, (rows, WIN_KEYS), 0) % BLOCK)
    scale = HEAD_DIM ** -0.5

    def block(n, carry):
        q0 = pl.multiple_of(n * BLOCK, BLOCK)
        k0 = pl.multiple_of(jnp.clip(q0 - BLOCK, 0, s - WIN_KEYS), BLOCK)
        valid = jnp.abs(key_minus_query + (k0 - q0)) <= WINDOW
        for kvh in range(N_KV_HEADS):
            heads = [kvh * GROUP + g for g in range(GROUP)]
            qg = jnp.concatenate(
                [q_ref[pl.ds(q0, BLOCK), h * HEAD_DIM:(h + 1) * HEAD_DIM] for h in heads], axis=0)
            k = kv_ref[pl.ds(k0, WIN_KEYS), kvh * HEAD_DIM:(kvh + 1) * HEAD_DIM]
            v = kv_ref[pl.ds(k0, WIN_KEYS), KV_WIDTH + kvh * HEAD_DIM:KV_WIDTH + (kvh + 1) * HEAD_DIM]
            sc = jnp.where(valid, _dot_nt(qg, k) * scale, NEG_INF)
            sink = jnp.concatenate([jnp.full((BLOCK, 1), sink_ref[h], F32) for h in heads], axis=0)
            m = jnp.maximum(jnp.max(sc, axis=1, keepdims=True), sink)
            e = jnp.exp(sc - m)
            denom = jnp.sum(e, axis=1, keepdims=True) + jnp.exp(sink - m)
            p = (e * (1.0 / denom)).astype(BF16)
            o = _dot(p, v)
            for g, h in enumerate(heads):
                o_ref[pl.ds(q0, BLOCK), h * HEAD_DIM:(h + 1) * HEAD_DIM] = (
                    o[g * BLOCK:(g + 1) * BLOCK].astype(o_ref.dtype))
        return carry

    lax.fori_loop(0, s // BLOCK, block, 0)


def _win_attention(qkvm, sink, batch, seq):
    x = qkvm.reshape(batch, seq, QKVM_WIDTH)
    kv_block = Q_WIDTH // (2 * KV_WIDTH)
    blocks = seq * (2 * Q_WIDTH + 2 * KV_WIDTH) * 2
    temps = 12 * GROUP * BLOCK * WIN_KEYS * 4
    return pl.pallas_call(
        _win_attn_kernel,
        grid=(batch,),
        in_specs=[
            pl.BlockSpec(memory_space=pltpu.SMEM),
            pl.BlockSpec((None, seq, Q_WIDTH), lambda i: (i, 0, 0)),
            pl.BlockSpec((None, seq, 2 * KV_WIDTH), lambda i: (i, 0, kv_block)),
        ],
        out_specs=pl.BlockSpec((None, seq, Q_WIDTH), lambda i: (i, 0, 0)),
        out_shape=jax.ShapeDtypeStruct((batch, seq, Q_WIDTH), BF16),
        compiler_params=_params(("parallel",), _vmem_limit(blocks, temps)),
        name="win_attn",
    )(sink, x, x)


MEM_ROWS = 512


def _mem_attn_kernel(q_ref, kv_ref, o_ref):
    scale = MEM_HEAD_DIM ** -0.5

    def chunk(i, carry):
        r0 = pl.multiple_of(i * MEM_ROWS, MEM_ROWS)
        for h in range(MEM_HEADS):
            cols = slice(h * MEM_HEAD_DIM, (h + 1) * MEM_HEAD_DIM)
            vcols = slice(MEM_WIDTH + h * MEM_HEAD_DIM, MEM_WIDTH + (h + 1) * MEM_HEAD_DIM)
            sc = _dot_nt(q_ref[pl.ds(r0, MEM_ROWS), cols], kv_ref[:, cols]) * scale
            e = jnp.exp(sc - jnp.max(sc, axis=1, keepdims=True))
            p = (e * (1.0 / jnp.sum(e, axis=1, keepdims=True))).astype(BF16)
            o_ref[pl.ds(r0, MEM_ROWS), cols] = _dot(p, kv_ref[:, vcols]).astype(o_ref.dtype)
        return carry

    lax.fori_loop(0, q_ref.shape[0] // MEM_ROWS, chunk, 0)


def _mem_attention(qkvm, mem_kv, batch, seq):
    x = qkvm.reshape(batch, seq, QKVM_WIDTH)
    kv = mem_kv.reshape(batch, N_MEM, 2 * MEM_WIDTH)
    half = MEM_WIDTH // 2
    first = (Q_WIDTH + 2 * KV_WIDTH) // half
    blocks = seq * 2 * MEM_WIDTH * 2 + N_MEM * 2 * MEM_WIDTH * 2
    temps = 12 * MEM_ROWS * N_MEM * 4

    def body(qa_ref, qb_ref, kv_ref, o_ref, q_ref):
        q_ref[:, :half] = qa_ref[...]
        q_ref[:, half:] = qb_ref[...]
        _mem_attn_kernel(q_ref, kv_ref, o_ref)

    return pl.pallas_call(
        body,
        grid=(batch,),
        in_specs=[
            pl.BlockSpec((None, seq, half), lambda i: (i, 0, first)),
            pl.BlockSpec((None, seq, half), lambda i: (i, 0, first + 1)),
            pl.BlockSpec((None, N_MEM, 2 * MEM_WIDTH), lambda i: (i, 0, 0)),
        ],
        out_specs=pl.BlockSpec((None, seq, MEM_WIDTH), lambda i: (i, 0, 0)),
        out_shape=jax.ShapeDtypeStruct((batch, seq, MEM_WIDTH), BF16),
        scratch_shapes=[pltpu.VMEM((seq, MEM_WIDTH), BF16)],
        compiler_params=_params(("parallel",), _vmem_limit(blocks, temps + seq * MEM_WIDTH * 2)),
        name="mem_attn",
    )(x, x, kv)


def _plain_kernel(x_ref, w_ref, o_ref, xb_ref):
    _cast_rows_once(x_ref, xb_ref)
    o_ref[...] = _dot(xb_ref[...], w_ref[...]).astype(o_ref.dtype)


MIX_TM = 256


def _mix_out_kernel(hc_ref, ow_ref, om_ref, gate_ref, h_ref, wc_ref, ww_ref, wm_ref, wo_ref,
                    g_ref, b_ref, o_ref):
    d = h_ref.shape[1]
    merged = gate_ref[:, 0:d].astype(F32) * _dot(hc_ref[...], wc_ref[...])
    merged += gate_ref[:, d:2 * d].astype(F32) * _dot(ow_ref[...], ww_ref[...])
    merged += gate_ref[:, 2 * d:3 * d].astype(F32) * _dot(om_ref[...], wm_ref[...])
    y = _dot(merged.astype(BF16), wo_ref[...])
    o_ref[...] = _layer_norm_rows(ALPHA * h_ref[...] + y, g_ref[...], b_ref[...])


def _mix_out(hc, ow, om, gates, h, w_conv, w_win, w_mem, w_out, ln_g, ln_b):
    t, d = h.shape
    rows = lambda width: pl.BlockSpec((MIX_TM, width), lambda i: (i, 0))
    resident = lambda shape: pl.BlockSpec(shape, lambda i: (0, 0), pipeline_mode=pl.Buffered(1))
    weight_bytes = (CONV_CH + Q_WIDTH + MEM_WIDTH + d) * d * 2
    blocks = MIX_TM * ((CONV_CH + Q_WIDTH + MEM_WIDTH) * 2 + N_BRANCH * d * 2 + 2 * d * 4)
    temps = weight_bytes + 6 * MIX_TM * d * 4
    return pl.pallas_call(
        _mix_out_kernel,
        grid=(t // MIX_TM,),
        in_specs=[
            rows(CONV_CH), rows(Q_WIDTH), rows(MEM_WIDTH), rows(N_BRANCH * d), rows(d),
            resident((CONV_CH, d)), resident((Q_WIDTH, d)), resident((MEM_WIDTH, d)), resident((d, d)),
            pl.BlockSpec((1, d), lambda i: (0, 0)), pl.BlockSpec((1, d), lambda i: (0, 0)),
        ],
        out_specs=rows(d),
        out_shape=jax.ShapeDtypeStruct((t, d), F32),
        compiler_params=_params(("parallel",), _vmem_limit(blocks, temps)),
        name="mix_out",
    )(hc, ow, om, gates, h, w_conv, w_win, w_mem, w_out, ln_g, ln_b)


def kernel(x, mem, ln1_g, ln1_b, ffn1_w_up, ffn1_w_down, w_in, conv_dw_w, conv_dw_b, conv_ln_g,
           conv_ln_b, conv_w_out, win_w_o, win_sink, mem_w_kv, mem_w_o, w_out, ln2_g, ln2_b,
           ffn2_w_up, ffn2_w_down, ln3_g, ln3_b):
    batch, seq, d = x.shape
    t = batch * seq
    cos_t, sin_lo, sin_hi = _rope_tables(seq)
    rope_specs = [pl.BlockSpec((PROJ_TM, HEAD_DIM), lambda i, j: (i % (seq // PROJ_TM), 0))] * 3
    row = lambda v: v.reshape(1, -1)
    mem2 = mem.reshape(batch * N_MEM, d)
    h = x.reshape(t, d)
    for l in range(DEPTH):
        bf = lambda w: w[l].astype(BF16)
        h = _ffn(h, bf(ffn1_w_up), bf(ffn1_w_down), row(ln1_g[l]), row(ln1_b[l]))

        w_in_l = bf(w_in)
        glu = _proj_call(_glu_kernel, h, [w_in_l, w_in_l],
                         [OFF_CONV // PROJ_TN, (OFF_CONV + CONV_CH) // PROJ_TN],
                         [], [], CONV_CH, F32, "proj_glu")
        qkvm = _proj_call(_qkvm_kernel, h, [w_in_l], [OFF_QKV // PROJ_TN],
                          [cos_t, sin_lo, sin_hi], rope_specs, QKVM_WIDTH, BF16, "proj_qkvm")
        gates = _proj_call(_gate_kernel, h, [w_in_l], [OFF_GATE // PROJ_TN],
                           [], [], N_BRANCH * d, BF16, "proj_gates")
        mem_kv = _proj_call(_plain_kernel, mem2, [bf(mem_w_kv)], [0],
                            [], [], 2 * MEM_WIDTH, BF16, "proj_mem_kv")

        hc = _conv_branch(glu.reshape(batch, seq, CONV_CH), conv_dw_w[l], row(conv_dw_b[l]),
                          row(conv_ln_g[l]), row(conv_ln_b[l]))
        ow = _win_attention(qkvm, win_sink[l], batch, seq)
        om = _mem_attention(qkvm, mem_kv, batch, seq)

        h = _mix_out(hc.reshape(t, CONV_CH), ow.reshape(t, Q_WIDTH), om.reshape(t, MEM_WIDTH),
                     gates, h, bf(conv_w_out), bf(win_w_o), bf(mem_w_o), bf(w_out),
                     row(ln2_g[l]), row(ln2_b[l]))

        h = _ffn(h, bf(ffn2_w_up), bf(ffn2_w_down), row(ln3_g[l]), row(ln3_b[l]))
    return h.reshape(batch, seq, d)
```

```python
import functools

import jax
import jax.numpy as jnp
from jax import lax
from jax.experimental import pallas as pl
from jax.experimental.pallas import tpu as pltpu

D_MODEL = 2048
DEPTH = 2
N_MEM = 256
CONV_CH = 1024
CONV_WIDTH = 31
CONV_HALF = CONV_WIDTH // 2
HEAD_DIM = 128
N_Q_HEADS = 8
N_KV_HEADS = 2
GROUP = N_Q_HEADS // N_KV_HEADS
Q_WIDTH = N_Q_HEADS * HEAD_DIM
KV_WIDTH = N_KV_HEADS * HEAD_DIM
WINDOW = 128
BLOCK = 128
ROT_DIM = HEAD_DIM // 4
ROT_HALF = ROT_DIM // 2
ROPE_THETA = 500000.0
MEM_HEADS = 4
MEM_HEAD_DIM = 256
MEM_WIDTH = MEM_HEADS * MEM_HEAD_DIM
N_BRANCH = 3
D_FF = 5632
ALPHA = (2 * DEPTH) ** 0.25
LN_EPS = 1e-5
NEG_INF = -1e30

OFF_CONV = 0
OFF_QKV = 2 * CONV_CH
QKVM_WIDTH = Q_WIDTH + 2 * KV_WIDTH + MEM_WIDTH
OFF_GATE = OFF_QKV + QKVM_WIDTH

V7X_LANES = 128
V7X_SUBLANES = 8
V7X_VMEM_BYTES = 64 * 1024 * 1024
VMEM_REQUEST_CAP = V7X_VMEM_BYTES - 8 * 1024 * 1024

F32 = jnp.float32
BF16 = jnp.bfloat16


def _vmem_limit(block_bytes, temp_bytes):
    need = int((2 * block_bytes + temp_bytes) * 1.25)
    return min(max(need, 16 * 1024 * 1024), VMEM_REQUEST_CAP)


def _params(semantics, vmem_bytes):
    return pltpu.CompilerParams(dimension_semantics=semantics, vmem_limit_bytes=vmem_bytes)


def _layer_norm_rows(z, g, b):
    mu = jnp.mean(z, axis=-1, keepdims=True)
    zc = z - mu
    var = jnp.mean(zc * zc, axis=-1, keepdims=True)
    return zc * lax.rsqrt(var + LN_EPS) * g + b


def _dot(a, b):
    return jnp.dot(a, b, preferred_element_type=F32)


def _dot_nt(a, b):
    return lax.dot_general(a, b, (((1,), (1,)), ((), ())), preferred_element_type=F32)


CAST_BLOCK_BYTES = 8 * 1024 * 1024


def _cast_kernel(w_ref, o_ref):
    o_ref[...] = w_ref[...].astype(o_ref.dtype)


def _cast_bf16(w):
    depth, rows, cols = w.shape
    rb = rows
    while rb * cols * 4 > CAST_BLOCK_BYTES and rb % 2 == 0 and (rb // 2) % 16 == 0:
        rb //= 2
    return pl.pallas_call(
        _cast_kernel,
        grid=(depth, rows // rb),
        in_specs=[pl.BlockSpec((None, rb, cols), lambda l, i: (l, i, 0))],
        out_specs=pl.BlockSpec((None, rb, cols), lambda l, i: (l, i, 0)),
        out_shape=jax.ShapeDtypeStruct(w.shape, BF16),
        compiler_params=_params(("parallel", "parallel"), _vmem_limit(rb * cols * 6, 0)),
        name="cast_bf16",
    )(w)


FFN_TM = 512
FFN_TF = 512


def _ffn_kernel(xm_ref, x_ref, wg_ref, wu_ref, wd_ref, g_ref, b_ref, o_ref, ob_ref):
    j = pl.program_id(1)

    @pl.when(j == 0)
    def _():
        o_ref[...] = jnp.zeros_like(o_ref)

    xb = xm_ref[...].astype(BF16)
    gate = _dot(xb, wg_ref[...])
    up = _dot(xb, wu_ref[...])
    h = (gate * jax.nn.sigmoid(gate) * up).astype(BF16)
    o_ref[...] += _dot(h, wd_ref[...])

    @pl.when(j == pl.num_programs(1) - 1)
    def _():
        y = _layer_norm_rows(ALPHA * x_ref[...] + 0.5 * o_ref[...], g_ref[...], b_ref[...])
        o_ref[...] = y
        ob_ref[...] = y.astype(ob_ref.dtype)


def _ffn(xm, x, w_up, w_down, layer, ln_g, ln_b):
    t, d = x.shape
    nf = D_FF // FFN_TF
    blocks = FFN_TM * d * (xm.dtype.itemsize + 4 + 4 + 2) + 3 * d * FFN_TF * 2
    temps = 4 * FFN_TM * FFN_TF * 4 + FFN_TM * d * 4
    rows = lambda: pl.BlockSpec((FFN_TM, d), lambda i, j: (i, 0))
    return pl.pallas_call(
        _ffn_kernel,
        grid=(t // FFN_TM, nf),
        in_specs=[
            rows(), rows(),
            pl.BlockSpec((None, d, FFN_TF), lambda i, j: (layer, 0, j)),
            pl.BlockSpec((None, d, FFN_TF), lambda i, j: (layer, 0, nf + j)),
            pl.BlockSpec((None, FFN_TF, d), lambda i, j: (layer, j, 0)),
            pl.BlockSpec((1, d), lambda i, j: (0, 0)),
            pl.BlockSpec((1, d), lambda i, j: (0, 0)),
        ],
        out_specs=[rows(), rows()],
        out_shape=[jax.ShapeDtypeStruct((t, d), F32), jax.ShapeDtypeStruct((t, d), BF16)],
        compiler_params=_params(("parallel", "arbitrary"), _vmem_limit(blocks, temps)),
        name="ffn",
    )(xm, x, w_up, w_up, w_down, ln_g, ln_b)


PROJ_TM = 1024


def _glu_kernel(x_ref, wa_ref, wg_ref, o_ref):
    x = x_ref[...]
    o_ref[...] = _dot(x, wa_ref[...]) * jax.nn.sigmoid(_dot(x, wg_ref[...]))


def _rope(r, cos, sin_lo, sin_hi):
    n = r.shape[1] // HEAD_DIM
    tile = lambda tbl: jnp.concatenate([tbl] * n, axis=1) if n > 1 else tbl
    from_lower = pltpu.roll(r, ROT_HALF, axis=1)
    from_upper = pltpu.roll(r, r.shape[1] - ROT_HALF, axis=1)
    return r * tile(cos) + from_lower * tile(sin_lo) + from_upper * tile(sin_hi)


QKVM_TN = 512
KV_TILE = Q_WIDTH // QKVM_TN
GATE_TN = 1536
MEM_KV_TN = 1024


def _qkvm_kernel(x_ref, w_ref, cos_ref, slo_ref, shi_ref, o_ref):
    j = pl.program_id(1)

    @pl.when(j < KV_TILE)
    def _():
        r = _dot(x_ref[...], w_ref[...])
        o_ref[...] = _rope(r, cos_ref[...], slo_ref[...], shi_ref[...]).astype(o_ref.dtype)

    @pl.when(j == KV_TILE)
    def _():
        r = _dot(x_ref[...], w_ref[...])
        k = _rope(r[:, :KV_WIDTH], cos_ref[...], slo_ref[...], shi_ref[...])
        o_ref[...] = jnp.concatenate([k, r[:, KV_WIDTH:]], axis=1).astype(o_ref.dtype)

    @pl.when(j > KV_TILE)
    def _():
        o_ref[...] = _dot(x_ref[...], w_ref[...]).astype(o_ref.dtype)


def _gate_kernel(x_ref, w_ref, o_ref):
    o_ref[...] = jax.nn.sigmoid(_dot(x_ref[...], w_ref[...])).astype(o_ref.dtype)


def _plain_kernel(x_ref, w_ref, o_ref):
    o_ref[...] = _dot(x_ref[...], w_ref[...]).astype(o_ref.dtype)


def _proj_call(kernel, x, weight, layer, col_offsets, tn, extra, extra_specs, n_out, out_dtype, name):
    t, d = x.shape
    out_bytes = jnp.dtype(out_dtype).itemsize
    n_w = len(col_offsets)
    blocks = PROJ_TM * d * 2 + n_w * d * tn * 2 + PROJ_TM * tn * out_bytes
    temps = (2 + 2 * n_w) * PROJ_TM * tn * 4
    assert all(off % tn == 0 for off in col_offsets) and n_out % tn == 0
    w_specs = [pl.BlockSpec((None, d, tn), functools.partial(lambda i, j, blk: (layer, 0, blk + j), blk=off // tn))
               for off in col_offsets]
    return pl.pallas_call(
        kernel,
        grid=(t // PROJ_TM, n_out // tn),
        in_specs=[pl.BlockSpec((PROJ_TM, d), lambda i, j: (i, 0))] + w_specs + extra_specs,
        out_specs=pl.BlockSpec((PROJ_TM, tn), lambda i, j: (i, j)),
        out_shape=jax.ShapeDtypeStruct((t, n_out), out_dtype),
        compiler_params=_params(("parallel", "arbitrary"), _vmem_limit(blocks, temps)),
        name=name,
    )(x, *([weight] * n_w), *extra)


def _rope_tables(seq_len):
    pos = jnp.arange(seq_len, dtype=F32)
    inv_freq = ROPE_THETA ** (-jnp.arange(0, ROT_DIM, 2, dtype=F32) / ROT_DIM)
    ang = pos[:, None] * inv_freq[None, :]
    cos, sin = jnp.cos(ang), jnp.sin(ang)
    rest = HEAD_DIM - ROT_DIM
    zeros = lambda n: jnp.zeros((seq_len, n), F32)
    cos_t = jnp.concatenate([cos, cos, jnp.ones((seq_len, rest), F32)], axis=1)
    sin_lo = jnp.concatenate([zeros(ROT_HALF), sin, zeros(rest)], axis=1)
    sin_hi = jnp.concatenate([-sin, zeros(ROT_HALF), zeros(rest)], axis=1)
    return cos_t, sin_lo, sin_hi


CONV_ROWS = 64
CONV_PAD = 16
CONV_SHIFT_ROWS = 208
CONV_TAP0 = CONV_PAD - CONV_HALF


def _conv_kernel(x_ref, w_ref, cb_ref, g_ref, b_ref, o_ref, sh_ref, cv_ref):
    s, c = x_ref.shape
    padded = s + 2 * CONV_PAD
    tail = sh_ref.shape[1] - (CONV_PAD + s)
    assert padded % CONV_SHIFT_ROWS == 0 and padded + V7X_SUBLANES <= sh_ref.shape[1]

    for ct in range(c // V7X_LANES):
        lanes = slice(ct * V7X_LANES, (ct + 1) * V7X_LANES)
        sh_ref[0, pl.ds(0, CONV_PAD), :] = jnp.zeros((CONV_PAD, V7X_LANES), F32)
        sh_ref[0, pl.ds(CONV_PAD + s, tail), :] = jnp.zeros((tail, V7X_LANES), F32)
        sh_ref[0, pl.ds(CONV_PAD, s), :] = x_ref[:, lanes]

        def shift(i, carry):
            r0 = pl.multiple_of(i * CONV_SHIFT_ROWS, V7X_SUBLANES)
            win = sh_ref[0, pl.ds(r0, CONV_SHIFT_ROWS + V7X_SUBLANES), :]
            for p in range(1, V7X_SUBLANES):
                sh_ref[p, pl.ds(r0, CONV_SHIFT_ROWS), :] = win[p:p + CONV_SHIFT_ROWS]
            return carry

        lax.fori_loop(0, padded // CONV_SHIFT_ROWS, shift, 0)

        def taps(i, carry):
            r0 = pl.multiple_of(i * CONV_ROWS, CONV_ROWS)
            acc = jnp.broadcast_to(cb_ref[:, lanes], (CONV_ROWS, V7X_LANES))
            for k in range(CONV_WIDTH):
                off = CONV_TAP0 + k
                rows = pl.ds(r0 + (off // V7X_SUBLANES) * V7X_SUBLANES, CONV_ROWS)
                acc = acc + sh_ref[off % V7X_SUBLANES, rows, :] * w_ref[k:k + 1, lanes]
            cv_ref[pl.ds(r0, CONV_ROWS), lanes] = acc
            return carry

        lax.fori_loop(0, s // CONV_ROWS, taps, 0)

    def norm(i, carry):
        r0 = pl.multiple_of(i * CONV_ROWS, CONV_ROWS)
        y = _layer_norm_rows(cv_ref[pl.ds(r0, CONV_ROWS), :], g_ref[...], b_ref[...])
        o_ref[pl.ds(r0, CONV_ROWS), :] = (y * jax.nn.sigmoid(y)).astype(o_ref.dtype)
        return carry

    lax.fori_loop(0, s // CONV_ROWS, norm, 0, unroll=2)


def _conv_branch(glu, dw_w, dw_b, ln_g, ln_b):
    b, s, c = glu.shape
    sh_rows = s + 2 * CONV_PAD + V7X_SUBLANES
    blocks = s * c * 4 + s * c * 2 + CONV_WIDTH * c * 4
    temps = V7X_SUBLANES * sh_rows * V7X_LANES * 4 + s * c * 4
    row = lambda: pl.BlockSpec((1, c), lambda i: (0, 0))
    return pl.pallas_call(
        _conv_kernel,
        grid=(b,),
        in_specs=[
            pl.BlockSpec((None, s, c), lambda i: (i, 0, 0)),
            pl.BlockSpec((CONV_WIDTH, c), lambda i: (0, 0)),
            row(), row(), row(),
        ],
        out_specs=pl.BlockSpec((None, s, c), lambda i: (i, 0, 0)),
        out_shape=jax.ShapeDtypeStruct((b, s, c), BF16),
        scratch_shapes=[pltpu.VMEM((V7X_SUBLANES, sh_rows, V7X_LANES), F32), pltpu.VMEM((s, c), F32)],
        compiler_params=_params(("parallel",), _vmem_limit(blocks, temps)),
        name="conv_branch",
    )(glu, dw_w, dw_b, ln_g, ln_b)


WIN_KEYS = 3 * BLOCK
WIN_BLOCKS = 4


def _masked_softmax_with_sink(sc, sink):
    m = jnp.maximum(jnp.max(sc, axis=1, keepdims=True), sink)
    e = jnp.exp(sc - m)
    denom = jnp.sum(e, axis=1, keepdims=True) + jnp.exp(sink - m)
    return e * (1.0 / denom)


def _win_attn_kernel(sink_ref, q_ref, kv_ref, o_ref):
    s = q_ref.shape[0]
    key_minus_query = (lax.broadcasted_iota(jnp.int32, (BLOCK, WIN_KEYS), 1)
                       - lax.broadcasted_iota(jnp.int32, (BLOCK, WIN_KEYS), 0))
    scale = HEAD_DIM ** -0.5

    def blocks(it, carry):
        chains = []
        for b in range(WIN_BLOCKS):
            q0 = pl.multiple_of((it * WIN_BLOCKS + b) * BLOCK, BLOCK)
            k0 = pl.multiple_of(jnp.clip(q0 - BLOCK, 0, s - WIN_KEYS), BLOCK)
            valid = jnp.abs(key_minus_query + (k0 - q0)) <= WINDOW
            for kvh in range(N_KV_HEADS):
                heads = [kvh * GROUP + g for g in range(GROUP)]
                qg = jnp.concatenate(
                    [q_ref[pl.ds(q0, BLOCK), h * HEAD_DIM:(h + 1) * HEAD_DIM] for h in heads], axis=0)
                k = kv_ref[pl.ds(k0, WIN_KEYS), kvh * HEAD_DIM:(kvh + 1) * HEAD_DIM]
                v = kv_ref[pl.ds(k0, WIN_KEYS),
                           KV_WIDTH + kvh * HEAD_DIM:KV_WIDTH + (kvh + 1) * HEAD_DIM]
                chains.append((q0, heads, valid, _dot_nt(qg, k), v))
        outs = []
        for _, heads, valid, sc_all, v in chains:
            p = [_masked_softmax_with_sink(
                     jnp.where(valid, sc_all[g * BLOCK:(g + 1) * BLOCK] * scale, NEG_INF), sink_ref[h])
                 for g, h in enumerate(heads)]
            outs.append(_dot(jnp.concatenate(p, axis=0).astype(BF16), v))
        for (q0, heads, _, _, _), o in zip(chains, outs):
            for g, h in enumerate(heads):
                o_ref[pl.ds(q0, BLOCK), h * HEAD_DIM:(h + 1) * HEAD_DIM] = (
                    o[g * BLOCK:(g + 1) * BLOCK].astype(o_ref.dtype))
        return carry

    lax.fori_loop(0, s // (BLOCK * WIN_BLOCKS), blocks, 0)


def _win_attention(qkvm, sink, batch, seq):
    x = qkvm.reshape(batch, seq, QKVM_WIDTH)
    kv_block = Q_WIDTH // (2 * KV_WIDTH)
    blocks = seq * (2 * Q_WIDTH + 2 * KV_WIDTH) * 2
    temps = 12 * WIN_BLOCKS * N_KV_HEADS * GROUP * BLOCK * WIN_KEYS * 4
    return pl.pallas_call(
        _win_attn_kernel,
        grid=(batch,),
        in_specs=[
            pl.BlockSpec(memory_space=pltpu.SMEM),
            pl.BlockSpec((None, seq, Q_WIDTH), lambda i: (i, 0, 0)),
            pl.BlockSpec((None, seq, 2 * KV_WIDTH), lambda i: (i, 0, kv_block)),
        ],
        out_specs=pl.BlockSpec((None, seq, Q_WIDTH), lambda i: (i, 0, 0)),
        out_shape=jax.ShapeDtypeStruct((batch, seq, Q_WIDTH), BF16),
        compiler_params=_params(("parallel",), _vmem_limit(blocks, temps)),
        name="win_attn",
    )(sink, x, x)


MEM_ROWS = 512


def _mem_attn_kernel(q_ref, kv_ref, o_ref):
    scale = MEM_HEAD_DIM ** -0.5

    def chunk(i, carry):
        r0 = pl.multiple_of(i * MEM_ROWS, MEM_ROWS)
        cols = [slice(h * MEM_HEAD_DIM, (h + 1) * MEM_HEAD_DIM) for h in range(MEM_HEADS)]
        scores = [_dot_nt(q_ref[pl.ds(r0, MEM_ROWS), c], kv_ref[:, c]) * scale for c in cols]
        outs = []
        for h, sc in enumerate(scores):
            e = jnp.exp(sc - jnp.max(sc, axis=1, keepdims=True))
            p = (e * (1.0 / jnp.sum(e, axis=1, keepdims=True))).astype(BF16)
            vcols = slice(MEM_WIDTH + h * MEM_HEAD_DIM, MEM_WIDTH + (h + 1) * MEM_HEAD_DIM)
            outs.append(_dot(p, kv_ref[:, vcols]))
        for c, o in zip(cols, outs):
            o_ref[pl.ds(r0, MEM_ROWS), c] = o.astype(o_ref.dtype)
        return carry

    lax.fori_loop(0, q_ref.shape[0] // MEM_ROWS, chunk, 0)


def _mem_attention(qkvm, mem_kv, batch, seq):
    x = qkvm.reshape(batch, seq, QKVM_WIDTH)
    kv = mem_kv.reshape(batch, N_MEM, 2 * MEM_WIDTH)
    half = MEM_WIDTH // 2
    first = (Q_WIDTH + 2 * KV_WIDTH) // half
    blocks = seq * 2 * MEM_WIDTH * 2 + N_MEM * 2 * MEM_WIDTH * 2
    temps = 12 * MEM_HEADS * MEM_ROWS * N_MEM * 4

    def body(qa_ref, qb_ref, kv_ref, o_ref, q_ref):
        q_ref[:, :half] = qa_ref[...]
        q_ref[:, half:] = qb_ref[...]
        _mem_attn_kernel(q_ref, kv_ref, o_ref)

    return pl.pallas_call(
        body,
        grid=(batch,),
        in_specs=[
            pl.BlockSpec((None, seq, half), lambda i: (i, 0, first)),
            pl.BlockSpec((None, seq, half), lambda i: (i, 0, first + 1)),
            pl.BlockSpec((None, N_MEM, 2 * MEM_WIDTH), lambda i: (i, 0, 0)),
        ],
        out_specs=pl.BlockSpec((None, seq, MEM_WIDTH), lambda i: (i, 0, 0)),
        out_shape=jax.ShapeDtypeStruct((batch, seq, MEM_WIDTH), BF16),
        scratch_shapes=[pltpu.VMEM((seq, MEM_WIDTH), BF16)],
        compiler_params=_params(("parallel",), _vmem_limit(blocks, temps + seq * MEM_WIDTH * 2)),
        name="mem_attn",
    )(x, x, kv)


MIX_TM = 256


def _mix_out_kernel(hc_ref, ow_ref, om_ref, gate_ref, h_ref, wc_ref, ww_ref, wm_ref, wo_ref,
                    g_ref, b_ref, o_ref, ob_ref):
    d = h_ref.shape[1]
    merged = gate_ref[:, 0:d].astype(F32) * _dot(hc_ref[...], wc_ref[...])
    merged += gate_ref[:, d:2 * d].astype(F32) * _dot(ow_ref[...], ww_ref[...])
    merged += gate_ref[:, 2 * d:3 * d].astype(F32) * _dot(om_ref[...], wm_ref[...])
    y = _dot(merged.astype(BF16), wo_ref[...])
    y = _layer_norm_rows(ALPHA * h_ref[...] + y, g_ref[...], b_ref[...])
    o_ref[...] = y
    ob_ref[...] = y.astype(ob_ref.dtype)


def _mix_out(hc, ow, om, gates, h, w_conv, w_win, w_mem, w_out, layer, ln_g, ln_b):
    t, d = h.shape
    rows = lambda width: pl.BlockSpec((MIX_TM, width), lambda i: (i, 0))
    resident = lambda k: pl.BlockSpec((None, k, d), lambda i: (layer, 0, 0), pipeline_mode=pl.Buffered(1))
    weight_bytes = (CONV_CH + Q_WIDTH + MEM_WIDTH + d) * d * 2
    blocks = MIX_TM * ((CONV_CH + Q_WIDTH + MEM_WIDTH) * 2 + N_BRANCH * d * 2 + 2 * d * 4 + d * 2)
    temps = weight_bytes + 6 * MIX_TM * d * 4
    return pl.pallas_call(
        _mix_out_kernel,
        grid=(t // MIX_TM,),
        in_specs=[
            rows(CONV_CH), rows(Q_WIDTH), rows(MEM_WIDTH), rows(N_BRANCH * d), rows(d),
            resident(CONV_CH), resident(Q_WIDTH), resident(MEM_WIDTH), resident(d),
            pl.BlockSpec((1, d), lambda i: (0, 0)), pl.BlockSpec((1, d), lambda i: (0, 0)),
        ],
        out_specs=[rows(d), rows(d)],
        out_shape=[jax.ShapeDtypeStruct((t, d), F32), jax.ShapeDtypeStruct((t, d), BF16)],
        compiler_params=_params(("parallel",), _vmem_limit(blocks, temps)),
        name="mix_out",
    )(hc, ow, om, gates, h, w_conv, w_win, w_mem, w_out, ln_g, ln_b)


def kernel(x, mem, ln1_g, ln1_b, ffn1_w_up, ffn1_w_down, w_in, conv_dw_w, conv_dw_b, conv_ln_g,
           conv_ln_b, conv_w_out, win_w_o, win_sink, mem_w_kv, mem_w_o, w_out, ln2_g, ln2_b,
           ffn2_w_up, ffn2_w_down, ln3_g, ln3_b):
    batch, seq, d = x.shape
    t = batch * seq
    cos_t, sin_lo, sin_hi = _rope_tables(seq)
    rope_specs = [pl.BlockSpec((PROJ_TM, HEAD_DIM), lambda i, j: (i % (seq // PROJ_TM), 0))] * 3
    row = lambda v: v.reshape(1, -1)
    mem_b = mem.reshape(batch * N_MEM, d).astype(BF16)

    ffn1_up, ffn1_down = _cast_bf16(ffn1_w_up), _cast_bf16(ffn1_w_down)
    ffn2_up, ffn2_down = _cast_bf16(ffn2_w_up), _cast_bf16(ffn2_w_down)
    w_in_b, mem_kv_b = _cast_bf16(w_in), _cast_bf16(mem_w_kv)
    w_conv_b, w_win_b = _cast_bf16(conv_w_out), _cast_bf16(win_w_o)
    w_mem_b, w_out_b = _cast_bf16(mem_w_o), _cast_bf16(w_out)

    h = x.reshape(t, d)
    hb = h
    for l in range(DEPTH):
        h, hb = _ffn(hb, h, ffn1_up, ffn1_down, l, row(ln1_g[l]), row(ln1_b[l]))

        glu = _proj_call(_glu_kernel, hb, w_in_b, l, [OFF_CONV, OFF_CONV + CONV_CH], CONV_CH,
                         [], [], CONV_CH, F32, "proj_glu")
        qkvm = _proj_call(_qkvm_kernel, hb, w_in_b, l, [OFF_QKV], QKVM_TN,
                          [cos_t, sin_lo, sin_hi], rope_specs, QKVM_WIDTH, BF16, "proj_qkvm")
        gates = _proj_call(_gate_kernel, hb, w_in_b, l, [OFF_GATE], GATE_TN,
                           [], [], N_BRANCH * d, BF16, "proj_gates")
        mem_kv = _proj_call(_plain_kernel, mem_b, mem_kv_b, l, [0], MEM_KV_TN,
                            [], [], 2 * MEM_WIDTH, BF16, "proj_mem_kv")

        hc = _conv_branch(glu.reshape(batch, seq, CONV_CH), conv_dw_w[l], row(conv_dw_b[l]),
                          row(conv_ln_g[l]), row(conv_ln_b[l]))
        ow = _win_attention(qkvm, win_sink[l], batch, seq)
        om = _mem_attention(qkvm, mem_kv, batch, seq)

        h, hb = _mix_out(hc.reshape(t, CONV_CH), ow.reshape(t, Q_WIDTH), om.reshape(t, MEM_WIDTH),
                         gates, h, w_conv_b, w_win_b, w_mem_b, w_out_b, l,
                         row(ln2_g[l]), row(ln2_b[l]))

        h, hb = _ffn(hb, h, ffn2_up, ffn2_down, l, row(ln3_g[l]), row(ln3_b[l]))
    return h.reshape(batch, seq, d)
```

```python
import functools

import jax
import jax.numpy as jnp
from jax import lax
from jax.experimental import pallas as pl
from jax.experimental.pallas import tpu as pltpu

D_MODEL = 2048
DEPTH = 2
N_MEM = 256
CONV_CH = 1024
CONV_WIDTH = 31
CONV_HALF = CONV_WIDTH // 2
HEAD_DIM = 128
N_Q_HEADS = 8
N_KV_HEADS = 2
GROUP = N_Q_HEADS // N_KV_HEADS
Q_WIDTH = N_Q_HEADS * HEAD_DIM
KV_WIDTH = N_KV_HEADS * HEAD_DIM
WINDOW = 128
BLOCK = 128
ROT_DIM = HEAD_DIM // 4
ROT_HALF = ROT_DIM // 2
ROPE_THETA = 500000.0
MEM_HEADS = 4
MEM_HEAD_DIM = 256
MEM_WIDTH = MEM_HEADS * MEM_HEAD_DIM
N_BRANCH = 3
D_FF = 5632
ALPHA = (2 * DEPTH) ** 0.25
LN_EPS = 1e-5
NEG_INF = -1e30

OFF_CONV = 0
OFF_QKV = 2 * CONV_CH
QKVM_WIDTH = Q_WIDTH + 2 * KV_WIDTH + MEM_WIDTH
OFF_GATE = OFF_QKV + QKVM_WIDTH

V7X_LANES = 128
V7X_SUBLANES = 8
V7X_VMEM_BYTES = 64 * 1024 * 1024
VMEM_REQUEST_CAP = V7X_VMEM_BYTES - 8 * 1024 * 1024

F32 = jnp.float32
BF16 = jnp.bfloat16


def _vmem_limit(block_bytes, temp_bytes):
    need = int((2 * block_bytes + temp_bytes) * 1.25)
    return min(max(need, 16 * 1024 * 1024), VMEM_REQUEST_CAP)


def _params(semantics, vmem_bytes):
    return pltpu.CompilerParams(dimension_semantics=semantics, vmem_limit_bytes=vmem_bytes)


def _layer_norm_rows(z, g, b):
    mu = jnp.mean(z, axis=-1, keepdims=True)
    zc = z - mu
    var = jnp.mean(zc * zc, axis=-1, keepdims=True)
    return zc * lax.rsqrt(var + LN_EPS) * g + b


def _dot(a, b):
    return jnp.dot(a, b, preferred_element_type=F32)


def _dot_nt(a, b):
    return lax.dot_general(a, b, (((1,), (1,)), ((), ())), preferred_element_type=F32)


CAST_BLOCK_BYTES = 8 * 1024 * 1024


def _cast_kernel(w_ref, o_ref):
    o_ref[...] = w_ref[...].astype(o_ref.dtype)


def _cast_bf16(w):
    depth, rows, cols = w.shape
    rb = rows
    while rb * cols * 4 > CAST_BLOCK_BYTES and rb % 2 == 0 and (rb // 2) % 16 == 0:
        rb //= 2
    return pl.pallas_call(
        _cast_kernel,
        grid=(depth, rows // rb),
        in_specs=[pl.BlockSpec((None, rb, cols), lambda l, i: (l, i, 0))],
        out_specs=pl.BlockSpec((None, rb, cols), lambda l, i: (l, i, 0)),
        out_shape=jax.ShapeDtypeStruct(w.shape, BF16),
        compiler_params=_params(("parallel", "parallel"), _vmem_limit(rb * cols * 6, 0)),
        name="cast_bf16",
    )(w)


FFN_TM = 1024
FFN_TF = 512
FFN_LN_ROWS = 256


def _ffn_kernel(xm_ref, x_hbm, wg_ref, wu_ref, wd_ref, g_ref, b_ref, o_hbm, ob_hbm,
                acc_ref, xy_ref, yb_ref, sems):
    i, j = pl.program_id(0), pl.program_id(1)
    last_i, last_j = pl.num_programs(0) - 1, pl.num_programs(1) - 1
    rows = pl.ds(pl.multiple_of(i * FFN_TM, FFN_TM), FFN_TM)
    x_in = pltpu.make_async_copy(x_hbm.at[rows], xy_ref, sems.at[0])
    y_out = pltpu.make_async_copy(xy_ref, o_hbm.at[rows], sems.at[1])
    yb_out = pltpu.make_async_copy(yb_ref, ob_hbm.at[rows], sems.at[2])

    @pl.when((i == 0) & (j == 0))
    def _():
        acc_ref[...] = jnp.zeros_like(acc_ref)

    @pl.when(j == 1)
    def _():
        @pl.when(i > 0)
        def _():
            y_out.wait()
            yb_out.wait()

        x_in.start()

    xb = xm_ref[...]
    gate = _dot(xb, wg_ref[...])
    up = _dot(xb, wu_ref[...])
    h = (gate * jax.nn.sigmoid(gate) * up).astype(BF16)
    acc_ref[...] += _dot(h, wd_ref[...])

    @pl.when(j == last_j)
    def _():
        x_in.wait()

        def norm(c, carry):
            r = pl.ds(pl.multiple_of(c * FFN_LN_ROWS, FFN_LN_ROWS), FFN_LN_ROWS)
            y = _layer_norm_rows(ALPHA * xy_ref[r, :] + 0.5 * acc_ref[r, :], g_ref[...], b_ref[...])
            xy_ref[r, :] = y
            yb_ref[r, :] = y.astype(yb_ref.dtype)
            acc_ref[r, :] = jnp.zeros((FFN_LN_ROWS, acc_ref.shape[1]), F32)
            return carry

        lax.fori_loop(0, FFN_TM // FFN_LN_ROWS, norm, 0)
        y_out.start()
        yb_out.start()

        @pl.when(i == last_i)
        def _():
            y_out.wait()
            yb_out.wait()


def _ffn(xm, x, w_up, w_down, layer, ln_g, ln_b):
    t, d = x.shape
    nf = D_FF // FFN_TF
    assert nf >= 3 and xm.dtype == BF16
    blocks = FFN_TM * d * 2 + 3 * d * FFN_TF * 2
    scratch = FFN_TM * d * (4 + 4 + 2)
    temps = scratch + 4 * FFN_TM * FFN_TF * 4 + 4 * FFN_LN_ROWS * d * 4
    hbm = lambda: pl.BlockSpec(memory_space=pl.ANY)
    return pl.pallas_call(
        _ffn_kernel,
        grid=(t // FFN_TM, nf),
        in_specs=[
            pl.BlockSpec((FFN_TM, d), lambda i, j: (i, 0)),
            hbm(),
            pl.BlockSpec((None, d, FFN_TF), lambda i, j: (layer, 0, j)),
            pl.BlockSpec((None, d, FFN_TF), lambda i, j: (layer, 0, nf + j)),
            pl.BlockSpec((None, FFN_TF, d), lambda i, j: (layer, j, 0)),
            pl.BlockSpec((1, d), lambda i, j: (0, 0)),
            pl.BlockSpec((1, d), lambda i, j: (0, 0)),
        ],
        out_specs=[hbm(), hbm()],
        out_shape=[jax.ShapeDtypeStruct((t, d), F32), jax.ShapeDtypeStruct((t, d), BF16)],
        scratch_shapes=[pltpu.VMEM((FFN_TM, d), F32), pltpu.VMEM((FFN_TM, d), F32),
                        pltpu.VMEM((FFN_TM, d), BF16), pltpu.SemaphoreType.DMA((3,))],
        compiler_params=_params(("arbitrary", "arbitrary"), _vmem_limit(blocks, temps)),
        name="ffn",
    )(xm, x, w_up, w_up, w_down, ln_g, ln_b)


PROJ_TM = 1024


def _glu_kernel(x_ref, wa_ref, wg_ref, o_ref):
    x = x_ref[...]
    o_ref[...] = _dot(x, wa_ref[...]) * jax.nn.sigmoid(_dot(x, wg_ref[...]))


def _rope(r, cos, sin_lo, sin_hi):
    n = r.shape[1] // HEAD_DIM
    tile = lambda tbl: jnp.concatenate([tbl] * n, axis=1) if n > 1 else tbl
    from_lower = pltpu.roll(r, ROT_HALF, axis=1)
    from_upper = pltpu.roll(r, r.shape[1] - ROT_HALF, axis=1)
    return r * tile(cos) + from_lower * tile(sin_lo) + from_upper * tile(sin_hi)


QKVM_TN = 512
KV_TILE = Q_WIDTH // QKVM_TN
GATE_TN = 1536
MEM_KV_TN = 1024


def _qkvm_kernel(x_ref, w_ref, cos_ref, slo_ref, shi_ref, o_ref):
    j = pl.program_id(1)

    @pl.when(j < KV_TILE)
    def _():
        r = _dot(x_ref[...], w_ref[...])
        o_ref[...] = _rope(r, cos_ref[...], slo_ref[...], shi_ref[...]).astype(o_ref.dtype)

    @pl.when(j == KV_TILE)
    def _():
        r = _dot(x_ref[...], w_ref[...])
        k = _rope(r[:, :KV_WIDTH], cos_ref[...], slo_ref[...], shi_ref[...])
        o_ref[...] = jnp.concatenate([k, r[:, KV_WIDTH:]], axis=1).astype(o_ref.dtype)

    @pl.when(j > KV_TILE)
    def _():
        o_ref[...] = _dot(x_ref[...], w_ref[...]).astype(o_ref.dtype)


def _gate_kernel(x_ref, w_ref, o_ref):
    o_ref[...] = jax.nn.sigmoid(_dot(x_ref[...], w_ref[...])).astype(o_ref.dtype)


def _plain_kernel(x_ref, w_ref, o_ref):
    o_ref[...] = _dot(x_ref[...], w_ref[...]).astype(o_ref.dtype)


def _proj_call(kernel, x, weight, layer, col_offsets, tn, extra, extra_specs, n_out, out_dtype, name):
    t, d = x.shape
    out_bytes = jnp.dtype(out_dtype).itemsize
    n_w = len(col_offsets)
    blocks = PROJ_TM * d * 2 + n_w * d * tn * 2 + PROJ_TM * tn * out_bytes
    temps = (2 + 2 * n_w) * PROJ_TM * tn * 4
    assert all(off % tn == 0 for off in col_offsets) and n_out % tn == 0
    w_specs = [pl.BlockSpec((None, d, tn), functools.partial(lambda i, j, blk: (layer, 0, blk + j), blk=off // tn))
               for off in col_offsets]
    return pl.pallas_call(
        kernel,
        grid=(t // PROJ_TM, n_out // tn),
        in_specs=[pl.BlockSpec((PROJ_TM, d), lambda i, j: (i, 0))] + w_specs + extra_specs,
        out_specs=pl.BlockSpec((PROJ_TM, tn), lambda i, j: (i, j)),
        out_shape=jax.ShapeDtypeStruct((t, n_out), out_dtype),
        compiler_params=_params(("parallel", "arbitrary"), _vmem_limit(blocks, temps)),
        name=name,
    )(x, *([weight] * n_w), *extra)


def _rope_tables(seq_len):
    pos = jnp.arange(seq_len, dtype=F32)
    inv_freq = ROPE_THETA ** (-jnp.arange(0, ROT_DIM, 2, dtype=F32) / ROT_DIM)
    ang = pos[:, None] * inv_freq[None, :]
    cos, sin = jnp.cos(ang), jnp.sin(ang)
    rest = HEAD_DIM - ROT_DIM
    zeros = lambda n: jnp.zeros((seq_len, n), F32)
    cos_t = jnp.concatenate([cos, cos, jnp.ones((seq_len, rest), F32)], axis=1)
    sin_lo = jnp.concatenate([zeros(ROT_HALF), sin, zeros(rest)], axis=1)
    sin_hi = jnp.concatenate([-sin, zeros(ROT_HALF), zeros(rest)], axis=1)
    return cos_t, sin_lo, sin_hi


CONV_ROWS = 64
CONV_PAD = 16
CONV_SHIFT_ROWS = 208
CONV_TAP0 = CONV_PAD - CONV_HALF


def _conv_kernel(x_ref, w_ref, cb_ref, g_ref, b_ref, o_ref, sh_ref, cv_ref):
    s, c = x_ref.shape
    padded = s + 2 * CONV_PAD
    tail = sh_ref.shape[1] - (CONV_PAD + s)
    assert padded % CONV_SHIFT_ROWS == 0 and padded + V7X_SUBLANES <= sh_ref.shape[1]

    for ct in range(c // V7X_LANES):
        lanes = slice(ct * V7X_LANES, (ct + 1) * V7X_LANES)
        sh_ref[0, pl.ds(0, CONV_PAD), :] = jnp.zeros((CONV_PAD, V7X_LANES), F32)
        sh_ref[0, pl.ds(CONV_PAD + s, tail), :] = jnp.zeros((tail, V7X_LANES), F32)
        sh_ref[0, pl.ds(CONV_PAD, s), :] = x_ref[:, lanes]

        def shift(i, carry):
            r0 = pl.multiple_of(i * CONV_SHIFT_ROWS, V7X_SUBLANES)
            win = sh_ref[0, pl.ds(r0, CONV_SHIFT_ROWS + V7X_SUBLANES), :]
            for p in range(1, V7X_SUBLANES):
                sh_ref[p, pl.ds(r0, CONV_SHIFT_ROWS), :] = win[p:p + CONV_SHIFT_ROWS]
            return carry

        lax.fori_loop(0, padded // CONV_SHIFT_ROWS, shift, 0)

        def taps(i, carry):
            r0 = pl.multiple_of(i * CONV_ROWS, CONV_ROWS)
            acc = jnp.broadcast_to(cb_ref[:, lanes], (CONV_ROWS, V7X_LANES))
            for k in range(CONV_WIDTH):
                off = CONV_TAP0 + k
                rows = pl.ds(r0 + (off // V7X_SUBLANES) * V7X_SUBLANES, CONV_ROWS)
                acc = acc + sh_ref[off % V7X_SUBLANES, rows, :] * w_ref[k:k + 1, lanes]
            cv_ref[pl.ds(r0, CONV_ROWS), lanes] = acc
            return carry

        lax.fori_loop(0, s // CONV_ROWS, taps, 0)

    def norm(i, carry):
        r0 = pl.multiple_of(i * CONV_ROWS, CONV_ROWS)
        y = _layer_norm_rows(cv_ref[pl.ds(r0, CONV_ROWS), :], g_ref[...], b_ref[...])
        o_ref[pl.ds(r0, CONV_ROWS), :] = (y * jax.nn.sigmoid(y)).astype(o_ref.dtype)
        return carry

    lax.fori_loop(0, s // CONV_ROWS, norm, 0, unroll=2)


def _conv_branch(glu, dw_w, dw_b, ln_g, ln_b):
    b, s, c = glu.shape
    sh_rows = s + 2 * CONV_PAD + V7X_SUBLANES
    blocks = s * c * 4 + s * c * 2 + CONV_WIDTH * c * 4
    temps = V7X_SUBLANES * sh_rows * V7X_LANES * 4 + s * c * 4
    row = lambda: pl.BlockSpec((1, c), lambda i: (0, 0))
    return pl.pallas_call(
        _conv_kernel,
        grid=(b,),
        in_specs=[
            pl.BlockSpec((None, s, c), lambda i: (i, 0, 0)),
            pl.BlockSpec((CONV_WIDTH, c), lambda i: (0, 0)),
            row(), row(), row(),
        ],
        out_specs=pl.BlockSpec((None, s, c), lambda i: (i, 0, 0)),
        out_shape=jax.ShapeDtypeStruct((b, s, c), BF16),
        scratch_shapes=[pltpu.VMEM((V7X_SUBLANES, sh_rows, V7X_LANES), F32), pltpu.VMEM((s, c), F32)],
        compiler_params=_params(("parallel",), _vmem_limit(blocks, temps)),
        name="conv_branch",
    )(glu, dw_w, dw_b, ln_g, ln_b)


WIN_KEYS = 3 * BLOCK
WIN_BLOCKS = 4


def _masked_softmax_with_sink(sc, sink):
    m = jnp.maximum(jnp.max(sc, axis=1, keepdims=True), sink)
    e = jnp.exp(sc - m)
    denom = jnp.sum(e, axis=1, keepdims=True) + jnp.exp(sink - m)
    return e * (1.0 / denom)


def _win_attn_kernel(sink_ref, q_ref, kv_ref, o_ref):
    s = q_ref.shape[0]
    key_minus_query = (lax.broadcasted_iota(jnp.int32, (BLOCK, WIN_KEYS), 1)
                       - lax.broadcasted_iota(jnp.int32, (BLOCK, WIN_KEYS), 0))
    scale = HEAD_DIM ** -0.5

    def blocks(it, carry):
        chains = []
        for b in range(WIN_BLOCKS):
            q0 = pl.multiple_of((it * WIN_BLOCKS + b) * BLOCK, BLOCK)
            k0 = pl.multiple_of(jnp.clip(q0 - BLOCK, 0, s - WIN_KEYS), BLOCK)
            valid = jnp.abs(key_minus_query + (k0 - q0)) <= WINDOW
            for kvh in range(N_KV_HEADS):
                heads = [kvh * GROUP + g for g in range(GROUP)]
                qg = jnp.concatenate(
                    [q_ref[pl.ds(q0, BLOCK), h * HEAD_DIM:(h + 1) * HEAD_DIM] for h in heads], axis=0)
                k = kv_ref[pl.ds(k0, WIN_KEYS), kvh * HEAD_DIM:(kvh + 1) * HEAD_DIM]
                v = kv_ref[pl.ds(k0, WIN_KEYS),
                           KV_WIDTH + kvh * HEAD_DIM:KV_WIDTH + (kvh + 1) * HEAD_DIM]
                chains.append((q0, heads, valid, _dot_nt(qg, k), v))
        outs = []
        for _, heads, valid, sc_all, v in chains:
            p = [_masked_softmax_with_sink(
                     jnp.where(valid, sc_all[g * BLOCK:(g + 1) * BLOCK] * scale, NEG_INF), sink_ref[h])
                 for g, h in enumerate(heads)]
            outs.append(_dot(jnp.concatenate(p, axis=0).astype(BF16), v))
        for (q0, heads, _, _, _), o in zip(chains, outs):
            for g, h in enumerate(heads):
                o_ref[pl.ds(q0, BLOCK), h * HEAD_DIM:(h + 1) * HEAD_DIM] = (
                    o[g * BLOCK:(g + 1) * BLOCK].astype(o_ref.dtype))
        return carry

    lax.fori_loop(0, s // (BLOCK * WIN_BLOCKS), blocks, 0)


def _win_attention(qkvm, sink, batch, seq):
    x = qkvm.reshape(batch, seq, QKVM_WIDTH)
    kv_block = Q_WIDTH // (2 * KV_WIDTH)
    blocks = seq * (2 * Q_WIDTH + 2 * KV_WIDTH) * 2
    temps = 12 * WIN_BLOCKS * N_KV_HEADS * GROUP * BLOCK * WIN_KEYS * 4
    return pl.pallas_call(
        _win_attn_kernel,
        grid=(batch,),
        in_specs=[
            pl.BlockSpec(memory_space=pltpu.SMEM),
            pl.BlockSpec((None, seq, Q_WIDTH), lambda i: (i, 0, 0)),
            pl.BlockSpec((None, seq, 2 * KV_WIDTH), lambda i: (i, 0, kv_block)),
        ],
        out_specs=pl.BlockSpec((None, seq, Q_WIDTH), lambda i: (i, 0, 0)),
        out_shape=jax.ShapeDtypeStruct((batch, seq, Q_WIDTH), BF16),
        compiler_params=_params(("parallel",), _vmem_limit(blocks, temps)),
        name="win_attn",
    )(sink, x, x)


MEM_ROWS = 512


def _mem_attn_kernel(q_ref, kv_ref, o_ref):
    scale = MEM_HEAD_DIM ** -0.5

    def chunk(i, carry):
        r0 = pl.multiple_of(i * MEM_ROWS, MEM_ROWS)
        cols = [slice(h * MEM_HEAD_DIM, (h + 1) * MEM_HEAD_DIM) for h in range(MEM_HEADS)]
        scores = [_dot_nt(q_ref[pl.ds(r0, MEM_ROWS), c], kv_ref[:, c]) * scale for c in cols]
        outs = []
        for h, sc in enumerate(scores):
            e = jnp.exp(sc - jnp.max(sc, axis=1, keepdims=True))
            p = (e * (1.0 / jnp.sum(e, axis=1, keepdims=True))).astype(BF16)
            vcols = slice(MEM_WIDTH + h * MEM_HEAD_DIM, MEM_WIDTH + (h + 1) * MEM_HEAD_DIM)
            outs.append(_dot(p, kv_ref[:, vcols]))
        for c, o in zip(cols, outs):
            o_ref[pl.ds(r0, MEM_ROWS), c] = o.astype(o_ref.dtype)
        return carry

    lax.fori_loop(0, q_ref.shape[0] // MEM_ROWS, chunk, 0)


def _mem_attention(qkvm, mem_kv, batch, seq):
    x = qkvm.reshape(batch, seq, QKVM_WIDTH)
    kv = mem_kv.reshape(batch, N_MEM, 2 * MEM_WIDTH)
    half = MEM_WIDTH // 2
    first = (Q_WIDTH + 2 * KV_WIDTH) // half
    blocks = seq * 2 * MEM_WIDTH * 2 + N_MEM * 2 * MEM_WIDTH * 2
    temps = 12 * MEM_HEADS * MEM_ROWS * N_MEM * 4

    def body(qa_ref, qb_ref, kv_ref, o_ref, q_ref):
        q_ref[:, :half] = qa_ref[...]
        q_ref[:, half:] = qb_ref[...]
        _mem_attn_kernel(q_ref, kv_ref, o_ref)

    return pl.pallas_call(
        body,
        grid=(batch,),
        in_specs=[
            pl.BlockSpec((None, seq, half), lambda i: (i, 0, first)),
            pl.BlockSpec((None, seq, half), lambda i: (i, 0, first + 1)),
            pl.BlockSpec((None, N_MEM, 2 * MEM_WIDTH), lambda i: (i, 0, 0)),
        ],
        out_specs=pl.BlockSpec((None, seq, MEM_WIDTH), lambda i: (i, 0, 0)),
        out_shape=jax.ShapeDtypeStruct((batch, seq, MEM_WIDTH), BF16),
        scratch_shapes=[pltpu.VMEM((seq, MEM_WIDTH), BF16)],
        compiler_params=_params(("parallel",), _vmem_limit(blocks, temps + seq * MEM_WIDTH * 2)),
        name="mem_attn",
    )(x, x, kv)


MIX_TM = 256


def _mix_out_kernel(hc_ref, ow_ref, om_ref, gate_ref, h_ref, wc_ref, ww_ref, wm_ref, wo_ref,
                    g_ref, b_ref, o_ref, ob_ref):
    d = h_ref.shape[1]
    merged = gate_ref[:, 0:d].astype(F32) * _dot(hc_ref[...], wc_ref[...])
    merged += gate_ref[:, d:2 * d].astype(F32) * _dot(ow_ref[...], ww_ref[...])
    merged += gate_ref[:, 2 * d:3 * d].astype(F32) * _dot(om_ref[...], wm_ref[...])
    y = _dot(merged.astype(BF16), wo_ref[...])
    y = _layer_norm_rows(ALPHA * h_ref[...] + y, g_ref[...], b_ref[...])
    o_ref[...] = y
    ob_ref[...] = y.astype(ob_ref.dtype)


def _mix_out(hc, ow, om, gates, h, w_conv, w_win, w_mem, w_out, layer, ln_g, ln_b):
    t, d = h.shape
    rows = lambda width: pl.BlockSpec((MIX_TM, width), lambda i: (i, 0))
    resident = lambda k: pl.BlockSpec((None, k, d), lambda i: (layer, 0, 0), pipeline_mode=pl.Buffered(1))
    weight_bytes = (CONV_CH + Q_WIDTH + MEM_WIDTH + d) * d * 2
    blocks = MIX_TM * ((CONV_CH + Q_WIDTH + MEM_WIDTH) * 2 + N_BRANCH * d * 2 + 2 * d * 4 + d * 2)
    temps = weight_bytes + 6 * MIX_TM * d * 4
    return pl.pallas_call(
        _mix_out_kernel,
        grid=(t // MIX_TM,),
        in_specs=[
            rows(CONV_CH), rows(Q_WIDTH), rows(MEM_WIDTH), rows(N_BRANCH * d), rows(d),
            resident(CONV_CH), resident(Q_WIDTH), resident(MEM_WIDTH), resident(d),
            pl.BlockSpec((1, d), lambda i: (0, 0)), pl.BlockSpec((1, d), lambda i: (0, 0)),
        ],
        out_specs=[rows(d), rows(d)],
        out_shape=[jax.ShapeDtypeStruct((t, d), F32), jax.ShapeDtypeStruct((t, d), BF16)],
        compiler_params=_params(("parallel",), _vmem_limit(blocks, temps)),
        name="mix_out",
    )(hc, ow, om, gates, h, w_conv, w_win, w_mem, w_out, ln_g, ln_b)


def kernel(x, mem, ln1_g, ln1_b, ffn1_w_up, ffn1_w_down, w_in, conv_dw_w, conv_dw_b, conv_ln_g,
           conv_ln_b, conv_w_out, win_w_o, win_sink, mem_w_kv, mem_w_o, w_out, ln2_g, ln2_b,
           ffn2_w_up, ffn2_w_down, ln3_g, ln3_b):
    batch, seq, d = x.shape
    t = batch * seq
    cos_t, sin_lo, sin_hi = _rope_tables(seq)
    rope_specs = [pl.BlockSpec((PROJ_TM, HEAD_DIM), lambda i, j: (i % (seq // PROJ_TM), 0))] * 3
    row = lambda v: v.reshape(1, -1)
    mem_b = mem.reshape(batch * N_MEM, d).astype(BF16)

    ffn1_up, ffn1_down = _cast_bf16(ffn1_w_up), _cast_bf16(ffn1_w_down)
    ffn2_up, ffn2_down = _cast_bf16(ffn2_w_up), _cast_bf16(ffn2_w_down)
    w_in_b, mem_kv_b = _cast_bf16(w_in), _cast_bf16(mem_w_kv)
    w_conv_b, w_win_b = _cast_bf16(conv_w_out), _cast_bf16(win_w_o)
    w_mem_b, w_out_b = _cast_bf16(mem_w_o), _cast_bf16(w_out)

    h = x.reshape(t, d)
    hb = _cast_bf16(x.reshape(1, t, d)).reshape(t, d)
    for l in range(DEPTH):
        h, hb = _ffn(hb, h, ffn1_up, ffn1_down, l, row(ln1_g[l]), row(ln1_b[l]))

        glu = _proj_call(_glu_kernel, hb, w_in_b, l, [OFF_CONV, OFF_CONV + CONV_CH], CONV_CH,
                         [], [], CONV_CH, F32, "proj_glu")
        qkvm = _proj_call(_qkvm_kernel, hb, w_in_b, l, [OFF_QKV], QKVM_TN,
                          [cos_t, sin_lo, sin_hi], rope_specs, QKVM_WIDTH, BF16, "proj_qkvm")
        gates = _proj_call(_gate_kernel, hb, w_in_b, l, [OFF_GATE], GATE_TN,
                           [], [], N_BRANCH * d, BF16, "proj_gates")
        mem_kv = _proj_call(_plain_kernel, mem_b, mem_kv_b, l, [0], MEM_KV_TN,
                            [], [], 2 * MEM_WIDTH, BF16, "proj_mem_kv")

        hc = _conv_branch(glu.reshape(batch, seq, CONV_CH), conv_dw_w[l], row(conv_dw_b[l]),
                          row(conv_ln_g[l]), row(conv_ln_b[l]))
        ow = _win_attention(qkvm, win_sink[l], batch, seq)
        om = _mem_attention(qkvm, mem_kv, batch, seq)

        h, hb = _mix_out(hc.reshape(t, CONV_CH), ow.reshape(t, Q_WIDTH), om.reshape(t, MEM_WIDTH),
                         gates, h, w_conv_b, w_win_b, w_mem_b, w_out_b, l,
                         row(ln2_g[l]), row(ln2_b[l]))

        h, hb = _ffn(hb, h, ffn2_up, ffn2_down, l, row(ln3_g[l]), row(ln3_b[l]))
    return h.reshape(batch, seq, d)
```

```python
import functools

import jax
import jax.numpy as jnp
from jax import lax
from jax.experimental import pallas as pl
from jax.experimental.pallas import tpu as pltpu

D_MODEL = 2048
DEPTH = 2
N_MEM = 256
CONV_CH = 1024
CONV_WIDTH = 31
CONV_HALF = CONV_WIDTH // 2
HEAD_DIM = 128
N_Q_HEADS = 8
N_KV_HEADS = 2
GROUP = N_Q_HEADS // N_KV_HEADS
Q_WIDTH = N_Q_HEADS * HEAD_DIM
KV_WIDTH = N_KV_HEADS * HEAD_DIM
WINDOW = 128
BLOCK = 128
ROT_DIM = HEAD_DIM // 4
ROT_HALF = ROT_DIM // 2
ROPE_THETA = 500000.0
MEM_HEADS = 4
MEM_HEAD_DIM = 256
MEM_WIDTH = MEM_HEADS * MEM_HEAD_DIM
N_BRANCH = 3
D_FF = 5632
ALPHA = (2 * DEPTH) ** 0.25
LN_EPS = 1e-5
NEG_INF = -1e30

OFF_CONV = 0
OFF_QKV = 2 * CONV_CH
QKVM_WIDTH = Q_WIDTH + 2 * KV_WIDTH + MEM_WIDTH
OFF_GATE = OFF_QKV + QKVM_WIDTH

V7X_LANES = 128
V7X_SUBLANES = 8
V7X_VMEM_BYTES = 64 * 1024 * 1024
VMEM_REQUEST_CAP = V7X_VMEM_BYTES - 8 * 1024 * 1024

F32 = jnp.float32
BF16 = jnp.bfloat16


def _vmem_limit(block_bytes, temp_bytes):
    need = int((2 * block_bytes + temp_bytes) * 1.25)
    return min(max(need, 16 * 1024 * 1024), VMEM_REQUEST_CAP)


def _params(semantics, vmem_bytes):
    return pltpu.CompilerParams(dimension_semantics=semantics, vmem_limit_bytes=vmem_bytes)


def _layer_norm_rows(z, g, b):
    mu = jnp.mean(z, axis=-1, keepdims=True)
    zc = z - mu
    var = jnp.mean(zc * zc, axis=-1, keepdims=True)
    return zc * lax.rsqrt(var + LN_EPS) * g + b


def _dot(a, b):
    return jnp.dot(a, b, preferred_element_type=F32)


def _dot_nt(a, b):
    return lax.dot_general(a, b, (((1,), (1,)), ((), ())), preferred_element_type=F32)


CAST_BLOCK_BYTES = 8 * 1024 * 1024


def _cast_kernel(w_ref, o_ref):
    o_ref[...] = w_ref[...].astype(o_ref.dtype)


def _cast_bf16(w):
    depth, rows, cols = w.shape
    rb = rows
    while rb * cols * 4 > CAST_BLOCK_BYTES and rb % 2 == 0 and (rb // 2) % 16 == 0:
        rb //= 2
    return pl.pallas_call(
        _cast_kernel,
        grid=(depth, rows // rb),
        in_specs=[pl.BlockSpec((None, rb, cols), lambda l, i: (l, i, 0))],
        out_specs=pl.BlockSpec((None, rb, cols), lambda l, i: (l, i, 0)),
        out_shape=jax.ShapeDtypeStruct(w.shape, BF16),
        compiler_params=_params(("parallel", "parallel"), _vmem_limit(rb * cols * 6, 0)),
        name="cast_bf16",
    )(w)


FFN_TM = 1024
FFN_TF = 512
FFN_LN_ROWS = 256


def _ffn_kernel(xm_ref, x_hbm, wg_ref, wu_ref, wd_ref, g_ref, b_ref, o_hbm, ob_hbm,
                acc_ref, xy_ref, yb_ref, sems):
    i, j = pl.program_id(0), pl.program_id(1)
    last_i, last_j = pl.num_programs(0) - 1, pl.num_programs(1) - 1
    rows = pl.ds(pl.multiple_of(i * FFN_TM, FFN_TM), FFN_TM)
    x_in = pltpu.make_async_copy(x_hbm.at[rows], xy_ref, sems.at[0])
    y_out = pltpu.make_async_copy(xy_ref, o_hbm.at[rows], sems.at[1])
    yb_out = pltpu.make_async_copy(yb_ref, ob_hbm.at[rows], sems.at[2])

    @pl.when((i == 0) & (j == 0))
    def _():
        acc_ref[...] = jnp.zeros_like(acc_ref)

    @pl.when(j == 1)
    def _():
        @pl.when(i > 0)
        def _():
            y_out.wait()
            yb_out.wait()

        x_in.start()

    xb = xm_ref[...]
    gate = _dot(xb, wg_ref[...])
    up = _dot(xb, wu_ref[...])
    h = (gate * jax.nn.sigmoid(gate) * up).astype(BF16)
    acc_ref[...] += _dot(h, wd_ref[...])

    @pl.when(j == last_j)
    def _():
        x_in.wait()

        def norm(c, carry):
            r = pl.ds(pl.multiple_of(c * FFN_LN_ROWS, FFN_LN_ROWS), FFN_LN_ROWS)
            y = _layer_norm_rows(ALPHA * xy_ref[r, :] + 0.5 * acc_ref[r, :], g_ref[...], b_ref[...])
            xy_ref[r, :] = y
            yb_ref[r, :] = y.astype(yb_ref.dtype)
            acc_ref[r, :] = jnp.zeros((FFN_LN_ROWS, acc_ref.shape[1]), F32)
            return carry

        lax.fori_loop(0, FFN_TM // FFN_LN_ROWS, norm, 0)
        y_out.start()
        yb_out.start()

        @pl.when(i == last_i)
        def _():
            y_out.wait()
            yb_out.wait()


def _ffn(xm, x, w_up, w_down, layer, ln_g, ln_b):
    t, d = x.shape
    nf = D_FF // FFN_TF
    assert nf >= 3 and xm.dtype == BF16
    blocks = FFN_TM * d * 2 + 3 * d * FFN_TF * 2
    scratch = FFN_TM * d * (4 + 4 + 2)
    temps = scratch + 4 * FFN_TM * FFN_TF * 4 + 4 * FFN_LN_ROWS * d * 4
    hbm = lambda: pl.BlockSpec(memory_space=pl.ANY)
    return pl.pallas_call(
        _ffn_kernel,
        grid=(t // FFN_TM, nf),
        in_specs=[
            pl.BlockSpec((FFN_TM, d), lambda i, j: (i, 0)),
            hbm(),
            pl.BlockSpec((None, d, FFN_TF), lambda i, j: (layer, 0, j)),
            pl.BlockSpec((None, d, FFN_TF), lambda i, j: (layer, 0, nf + j)),
            pl.BlockSpec((None, FFN_TF, d), lambda i, j: (layer, j, 0)),
            pl.BlockSpec((1, d), lambda i, j: (0, 0)),
            pl.BlockSpec((1, d), lambda i, j: (0, 0)),
        ],
        out_specs=[hbm(), hbm()],
        out_shape=[jax.ShapeDtypeStruct((t, d), F32), jax.ShapeDtypeStruct((t, d), BF16)],
        scratch_shapes=[pltpu.VMEM((FFN_TM, d), F32), pltpu.VMEM((FFN_TM, d), F32),
                        pltpu.VMEM((FFN_TM, d), BF16), pltpu.SemaphoreType.DMA((3,))],
        compiler_params=_params(("arbitrary", "arbitrary"), _vmem_limit(blocks, temps)),
        name="ffn",
    )(xm, x, w_up, w_up, w_down, ln_g, ln_b)


PROJ_TM = 1024


def _glu_kernel(x_ref, wa_ref, wg_ref, o_ref):
    x = x_ref[...]
    o_ref[...] = _dot(x, wa_ref[...]) * jax.nn.sigmoid(_dot(x, wg_ref[...]))


def _rope(r, cos, sin_lo, sin_hi):
    n = r.shape[1] // HEAD_DIM
    tile = lambda tbl: jnp.concatenate([tbl] * n, axis=1) if n > 1 else tbl
    from_lower = pltpu.roll(r, ROT_HALF, axis=1)
    from_upper = pltpu.roll(r, r.shape[1] - ROT_HALF, axis=1)
    return r * tile(cos) + from_lower * tile(sin_lo) + from_upper * tile(sin_hi)


QKVM_TN = 512
KV_TILE = Q_WIDTH // QKVM_TN
GATE_TN = 1536
MEM_KV_TN = 1024


def _qkvm_kernel(x_ref, w_ref, cos_ref, slo_ref, shi_ref, o_ref):
    j = pl.program_id(1)

    @pl.when(j < KV_TILE)
    def _():
        r = _dot(x_ref[...], w_ref[...])
        o_ref[...] = _rope(r, cos_ref[...], slo_ref[...], shi_ref[...]).astype(o_ref.dtype)

    @pl.when(j == KV_TILE)
    def _():
        r = _dot(x_ref[...], w_ref[...])
        k = _rope(r[:, :KV_WIDTH], cos_ref[...], slo_ref[...], shi_ref[...])
        o_ref[...] = jnp.concatenate([k, r[:, KV_WIDTH:]], axis=1).astype(o_ref.dtype)

    @pl.when(j > KV_TILE)
    def _():
        o_ref[...] = _dot(x_ref[...], w_ref[...]).astype(o_ref.dtype)


def _plain_kernel(x_ref, w_ref, o_ref):
    o_ref[...] = _dot(x_ref[...], w_ref[...]).astype(o_ref.dtype)


def _proj_call(kernel, x, weight, layer, col_offsets, tn, extra, extra_specs, n_out, out_dtype, name):
    t, d = x.shape
    out_bytes = jnp.dtype(out_dtype).itemsize
    n_w = len(col_offsets)
    blocks = PROJ_TM * d * 2 + n_w * d * tn * 2 + PROJ_TM * tn * out_bytes
    temps = (2 + 2 * n_w) * PROJ_TM * tn * 4
    assert all(off % tn == 0 for off in col_offsets) and n_out % tn == 0
    w_specs = [pl.BlockSpec((None, d, tn), functools.partial(lambda i, j, blk: (layer, 0, blk + j), blk=off // tn))
               for off in col_offsets]
    return pl.pallas_call(
        kernel,
        grid=(t // PROJ_TM, n_out // tn),
        in_specs=[pl.BlockSpec((PROJ_TM, d), lambda i, j: (i, 0))] + w_specs + extra_specs,
        out_specs=pl.BlockSpec((PROJ_TM, tn), lambda i, j: (i, j)),
        out_shape=jax.ShapeDtypeStruct((t, n_out), out_dtype),
        compiler_params=_params(("parallel", "arbitrary"), _vmem_limit(blocks, temps)),
        name=name,
    )(x, *([weight] * n_w), *extra)


def _rope_tables(seq_len):
    pos = jnp.arange(seq_len, dtype=F32)
    inv_freq = ROPE_THETA ** (-jnp.arange(0, ROT_DIM, 2, dtype=F32) / ROT_DIM)
    ang = pos[:, None] * inv_freq[None, :]
    cos, sin = jnp.cos(ang), jnp.sin(ang)
    rest = HEAD_DIM - ROT_DIM
    zeros = lambda n: jnp.zeros((seq_len, n), F32)
    cos_t = jnp.concatenate([cos, cos, jnp.ones((seq_len, rest), F32)], axis=1)
    sin_lo = jnp.concatenate([zeros(ROT_HALF), sin, zeros(rest)], axis=1)
    sin_hi = jnp.concatenate([-sin, zeros(ROT_HALF), zeros(rest)], axis=1)
    return cos_t, sin_lo, sin_hi


CONV_TILE = 256
CONV_HALO = 16
CONV_WIN = CONV_TILE + 2 * CONV_HALO
CONV_ROWS = 64
CONV_TAP0 = CONV_HALO - CONV_HALF
CONV_SH_ROWS = CONV_TILE + (CONV_TAP0 + CONV_WIDTH - 1) // V7X_SUBLANES * V7X_SUBLANES


def _conv_window_dma(glu_hbm, win_ref, sems, chunk, slot, n_chunks, act):
    body = CONV_WIN - CONV_HALO
    total = glu_hbm.shape[0]

    @pl.when(chunk == 0)
    def _():
        act(pltpu.make_async_copy(glu_hbm.at[pl.ds(0, body)],
                                  win_ref.at[slot, pl.ds(CONV_HALO, body)], sems.at[slot]))

    @pl.when(chunk == n_chunks - 1)
    def _():
        act(pltpu.make_async_copy(glu_hbm.at[pl.ds(total - body, body)],
                                  win_ref.at[slot, pl.ds(0, body)], sems.at[slot]))

    @pl.when((chunk > 0) & (chunk < n_chunks - 1))
    def _():
        start = pl.multiple_of(chunk * CONV_TILE - CONV_HALO, CONV_HALO)
        act(pltpu.make_async_copy(glu_hbm.at[pl.ds(start, CONV_WIN)], win_ref.at[slot], sems.at[slot]))


def _gate_conv_kernel(chunks_per_seq, x_ref, w_ref, glu_hbm, cw_ref, cb_ref, g_ref, b_ref,
                      gate_ref, hc_ref, win_ref, sh_ref, cv_ref, sems):
    i, j = pl.program_id(0), pl.program_id(1)
    n_chunks = pl.num_programs(0) * pl.num_programs(1)
    chunk = i * pl.num_programs(1) + j
    slot = chunk % 2
    c = cv_ref.shape[1]

    @pl.when(chunk == 0)
    def _():
        win_ref[0, pl.ds(0, CONV_HALO), :] = jnp.zeros((CONV_HALO, c), F32)
        _conv_window_dma(glu_hbm, win_ref, sems, chunk, slot, n_chunks, lambda cp: cp.start())

    _conv_window_dma(glu_hbm, win_ref, sems, chunk, slot, n_chunks, lambda cp: cp.wait())

    @pl.when(chunk == n_chunks - 1)
    def _():
        win_ref[slot, pl.ds(CONV_WIN - CONV_HALO, CONV_HALO), :] = jnp.zeros((CONV_HALO, c), F32)

    @pl.when(chunk + 1 < n_chunks)
    def _():
        _conv_window_dma(glu_hbm, win_ref, sems, chunk + 1, 1 - slot, n_chunks, lambda cp: cp.start())

    pos = chunk % chunks_per_seq
    row = lax.broadcasted_iota(jnp.int32, (CONV_WIN, V7X_LANES), 0)
    keep = ((row >= jnp.where(pos == 0, CONV_HALO, 0))
            & (row < jnp.where(pos == chunks_per_seq - 1, CONV_WIN - CONV_HALO, CONV_WIN)))

    for ct in range(c // V7X_LANES):
        lanes = slice(ct * V7X_LANES, (ct + 1) * V7X_LANES)
        win = jnp.where(keep, win_ref[slot, :, lanes], 0.0)
        sh = sh_ref.at[ct % 2]
        for p in range(V7X_SUBLANES):
            sh[p] = win[p:p + CONV_SH_ROWS]
        for r0 in range(0, CONV_TILE, CONV_ROWS):
            acc = jnp.broadcast_to(cb_ref[:, lanes], (CONV_ROWS, V7X_LANES))
            for k in range(CONV_WIDTH):
                off = CONV_TAP0 + k
                rows = pl.ds(r0 + off // V7X_SUBLANES * V7X_SUBLANES, CONV_ROWS)
                acc = acc + sh[off % V7X_SUBLANES, rows, :] * cw_ref[k:k + 1, lanes]
            cv_ref[pl.ds(r0, CONV_ROWS), lanes] = acc
    y = _layer_norm_rows(cv_ref[...], g_ref[...], b_ref[...])
    hc_ref[pl.ds(pl.multiple_of(j * CONV_TILE, CONV_TILE), CONV_TILE), :] = (
        y * jax.nn.sigmoid(y)).astype(hc_ref.dtype)

    gate_ref[...] = jax.nn.sigmoid(_dot(x_ref[...], w_ref[...])).astype(gate_ref.dtype)


def _gates_and_conv(x, w_in, layer, glu, dw_w, dw_b, ln_g, ln_b, seq):
    t, d = x.shape
    c = glu.shape[1]
    n_gate = N_BRANCH * d
    steps = n_gate // GATE_TN
    assert PROJ_TM == steps * CONV_TILE and seq % CONV_TILE == 0 and OFF_GATE % GATE_TN == 0
    assert CONV_SH_ROWS + V7X_SUBLANES - 1 <= CONV_WIN
    blocks = PROJ_TM * d * 2 + d * GATE_TN * 2 + PROJ_TM * GATE_TN * 2 + PROJ_TM * c * 2
    scratch = (2 * CONV_WIN * c + 2 * V7X_SUBLANES * CONV_SH_ROWS * V7X_LANES + CONV_TILE * c) * 4
    temps = scratch + 3 * PROJ_TM * GATE_TN * 4
    small = lambda r: pl.BlockSpec((r, c), lambda i, j: (0, 0))
    return pl.pallas_call(
        functools.partial(_gate_conv_kernel, seq // CONV_TILE),
        grid=(t // PROJ_TM, steps),
        in_specs=[
            pl.BlockSpec((PROJ_TM, d), lambda i, j: (i, 0)),
            pl.BlockSpec((None, d, GATE_TN), lambda i, j: (layer, 0, OFF_GATE // GATE_TN + j)),
            pl.BlockSpec(memory_space=pl.ANY),
            small(CONV_WIDTH), small(1), small(1), small(1),
        ],
        out_specs=[pl.BlockSpec((PROJ_TM, GATE_TN), lambda i, j: (i, j)),
                   pl.BlockSpec((PROJ_TM, c), lambda i, j: (i, 0))],
        out_shape=[jax.ShapeDtypeStruct((t, n_gate), BF16), jax.ShapeDtypeStruct((t, c), BF16)],
        scratch_shapes=[pltpu.VMEM((2, CONV_WIN, c), F32),
                        pltpu.VMEM((2, V7X_SUBLANES, CONV_SH_ROWS, V7X_LANES), F32),
                        pltpu.VMEM((CONV_TILE, c), F32),
                        pltpu.SemaphoreType.DMA((2,))],
        compiler_params=_params(("arbitrary", "arbitrary"), _vmem_limit(blocks, temps)),
        name="gates_conv",
    )(x, w_in, glu, dw_w, dw_b, ln_g, ln_b)


WIN_KEYS = 3 * BLOCK
WIN_BLOCKS = 4


def _masked_softmax_with_sink(sc, sink):
    m = jnp.maximum(jnp.max(sc, axis=1, keepdims=True), sink)
    e = jnp.exp(sc - m)
    denom = jnp.sum(e, axis=1, keepdims=True) + jnp.exp(sink - m)
    return e * (1.0 / denom)


def _win_attn_kernel(sink_ref, q_ref, kv_ref, o_ref):
    s = q_ref.shape[0]
    key_minus_query = (lax.broadcasted_iota(jnp.int32, (BLOCK, WIN_KEYS), 1)
                       - lax.broadcasted_iota(jnp.int32, (BLOCK, WIN_KEYS), 0))
    scale = HEAD_DIM ** -0.5

    def blocks(it, carry):
        chains = []
        for b in range(WIN_BLOCKS):
            q0 = pl.multiple_of((it * WIN_BLOCKS + b) * BLOCK, BLOCK)
            k0 = pl.multiple_of(jnp.clip(q0 - BLOCK, 0, s - WIN_KEYS), BLOCK)
            valid = jnp.abs(key_minus_query + (k0 - q0)) <= WINDOW
            for kvh in range(N_KV_HEADS):
                heads = [kvh * GROUP + g for g in range(GROUP)]
                qg = jnp.concatenate(
                    [q_ref[pl.ds(q0, BLOCK), h * HEAD_DIM:(h + 1) * HEAD_DIM] for h in heads], axis=0)
                k = kv_ref[pl.ds(k0, WIN_KEYS), kvh * HEAD_DIM:(kvh + 1) * HEAD_DIM]
                v = kv_ref[pl.ds(k0, WIN_KEYS),
                           KV_WIDTH + kvh * HEAD_DIM:KV_WIDTH + (kvh + 1) * HEAD_DIM]
                chains.append((q0, heads, valid, _dot_nt(qg, k), v))
        outs = []
        for _, heads, valid, sc_all, v in chains:
            p = [_masked_softmax_with_sink(
                     jnp.where(valid, sc_all[g * BLOCK:(g + 1) * BLOCK] * scale, NEG_INF), sink_ref[h])
                 for g, h in enumerate(heads)]
            outs.append(_dot(jnp.concatenate(p, axis=0).astype(BF16), v))
        for (q0, heads, _, _, _), o in zip(chains, outs):
            for g, h in enumerate(heads):
                o_ref[pl.ds(q0, BLOCK), h * HEAD_DIM:(h + 1) * HEAD_DIM] = (
                    o[g * BLOCK:(g + 1) * BLOCK].astype(o_ref.dtype))
        return carry

    lax.fori_loop(0, s // (BLOCK * WIN_BLOCKS), blocks, 0)


def _win_attention(qkvm, sink, batch, seq):
    x = qkvm.reshape(batch, seq, QKVM_WIDTH)
    kv_block = Q_WIDTH // (2 * KV_WIDTH)
    blocks = seq * (2 * Q_WIDTH + 2 * KV_WIDTH) * 2
    temps = 12 * WIN_BLOCKS * N_KV_HEADS * GROUP * BLOCK * WIN_KEYS * 4
    return pl.pallas_call(
        _win_attn_kernel,
        grid=(batch,),
        in_specs=[
            pl.BlockSpec(memory_space=pltpu.SMEM),
            pl.BlockSpec((None, seq, Q_WIDTH), lambda i: (i, 0, 0)),
            pl.BlockSpec((None, seq, 2 * KV_WIDTH), lambda i: (i, 0, kv_block)),
        ],
        out_specs=pl.BlockSpec((None, seq, Q_WIDTH), lambda i: (i, 0, 0)),
        out_shape=jax.ShapeDtypeStruct((batch, seq, Q_WIDTH), BF16),
        compiler_params=_params(("parallel",), _vmem_limit(blocks, temps)),
        name="win_attn",
    )(sink, x, x)


MEM_ROWS = 512


def _mem_attn_kernel(q_ref, kv_ref, o_ref):
    scale = MEM_HEAD_DIM ** -0.5

    def chunk(i, carry):
        r0 = pl.multiple_of(i * MEM_ROWS, MEM_ROWS)
        cols = [slice(h * MEM_HEAD_DIM, (h + 1) * MEM_HEAD_DIM) for h in range(MEM_HEADS)]
        scores = [_dot_nt(q_ref[pl.ds(r0, MEM_ROWS), c], kv_ref[:, c]) * scale for c in cols]
        outs = []
        for h, sc in enumerate(scores):
            e = jnp.exp(sc - jnp.max(sc, axis=1, keepdims=True))
            p = (e * (1.0 / jnp.sum(e, axis=1, keepdims=True))).astype(BF16)
            vcols = slice(MEM_WIDTH + h * MEM_HEAD_DIM, MEM_WIDTH + (h + 1) * MEM_HEAD_DIM)
            outs.append(_dot(p, kv_ref[:, vcols]))
        for c, o in zip(cols, outs):
            o_ref[pl.ds(r0, MEM_ROWS), c] = o.astype(o_ref.dtype)
        return carry

    lax.fori_loop(0, q_ref.shape[0] // MEM_ROWS, chunk, 0)


def _mem_attention(qkvm, mem_kv, batch, seq):
    x = qkvm.reshape(batch, seq, QKVM_WIDTH)
    kv = mem_kv.reshape(batch, N_MEM, 2 * MEM_WIDTH)
    half = MEM_WIDTH // 2
    first = (Q_WIDTH + 2 * KV_WIDTH) // half
    blocks = seq * 2 * MEM_WIDTH * 2 + N_MEM * 2 * MEM_WIDTH * 2
    temps = 12 * MEM_HEADS * MEM_ROWS * N_MEM * 4

    def body(qa_ref, qb_ref, kv_ref, o_ref, q_ref):
        q_ref[:, :half] = qa_ref[...]
        q_ref[:, half:] = qb_ref[...]
        _mem_attn_kernel(q_ref, kv_ref, o_ref)

    return pl.pallas_call(
        body,
        grid=(batch,),
        in_specs=[
            pl.BlockSpec((None, seq, half), lambda i: (i, 0, first)),
            pl.BlockSpec((None, seq, half), lambda i: (i, 0, first + 1)),
            pl.BlockSpec((None, N_MEM, 2 * MEM_WIDTH), lambda i: (i, 0, 0)),
        ],
        out_specs=pl.BlockSpec((None, seq, MEM_WIDTH), lambda i: (i, 0, 0)),
        out_shape=jax.ShapeDtypeStruct((batch, seq, MEM_WIDTH), BF16),
        scratch_shapes=[pltpu.VMEM((seq, MEM_WIDTH), BF16)],
        compiler_params=_params(("parallel",), _vmem_limit(blocks, temps + seq * MEM_WIDTH * 2)),
        name="mem_attn",
    )(x, x, kv)


MIX_TM = 256


def _mix_out_kernel(hc_ref, ow_ref, om_ref, gate_ref, h_ref, wc_ref, ww_ref, wm_ref, wo_ref,
                    g_ref, b_ref, o_ref, ob_ref):
    d = h_ref.shape[1]
    merged = gate_ref[:, 0:d].astype(F32) * _dot(hc_ref[...], wc_ref[...])
    merged += gate_ref[:, d:2 * d].astype(F32) * _dot(ow_ref[...], ww_ref[...])
    merged += gate_ref[:, 2 * d:3 * d].astype(F32) * _dot(om_ref[...], wm_ref[...])
    y = _dot(merged.astype(BF16), wo_ref[...])
    y = _layer_norm_rows(ALPHA * h_ref[...] + y, g_ref[...], b_ref[...])
    o_ref[...] = y
    ob_ref[...] = y.astype(ob_ref.dtype)


def _mix_out(hc, ow, om, gates, h, w_conv, w_win, w_mem, w_out, layer, ln_g, ln_b):
    t, d = h.shape
    rows = lambda width: pl.BlockSpec((MIX_TM, width), lambda i: (i, 0))
    resident = lambda k: pl.BlockSpec((None, k, d), lambda i: (layer, 0, 0), pipeline_mode=pl.Buffered(1))
    weight_bytes = (CONV_CH + Q_WIDTH + MEM_WIDTH + d) * d * 2
    blocks = MIX_TM * ((CONV_CH + Q_WIDTH + MEM_WIDTH) * 2 + N_BRANCH * d * 2 + 2 * d * 4 + d * 2)
    temps = weight_bytes + 6 * MIX_TM * d * 4
    return pl.pallas_call(
        _mix_out_kernel,
        grid=(t // MIX_TM,),
        in_specs=[
            rows(CONV_CH), rows(Q_WIDTH), rows(MEM_WIDTH), rows(N_BRANCH * d), rows(d),
            resident(CONV_CH), resident(Q_WIDTH), resident(MEM_WIDTH), resident(d),
            pl.BlockSpec((1, d), lambda i: (0, 0)), pl.BlockSpec((1, d), lambda i: (0, 0)),
        ],
        out_specs=[rows(d), rows(d)],
        out_shape=[jax.ShapeDtypeStruct((t, d), F32), jax.ShapeDtypeStruct((t, d), BF16)],
        compiler_params=_params(("parallel",), _vmem_limit(blocks, temps)),
        name="mix_out",
    )(hc, ow, om, gates, h, w_conv, w_win, w_mem, w_out, ln_g, ln_b)


def kernel(x, mem, ln1_g, ln1_b, ffn1_w_up, ffn1_w_down, w_in, conv_dw_w, conv_dw_b, conv_ln_g,
           conv_ln_b, conv_w_out, win_w_o, win_sink, mem_w_kv, mem_w_o, w_out, ln2_g, ln2_b,
           ffn2_w_up, ffn2_w_down, ln3_g, ln3_b):
    batch, seq, d = x.shape
    t = batch * seq
    cos_t, sin_lo, sin_hi = _rope_tables(seq)
    rope_specs = [pl.BlockSpec((PROJ_TM, HEAD_DIM), lambda i, j: (i % (seq // PROJ_TM), 0))] * 3
    row = lambda v: v.reshape(1, -1)
    mem_b = mem.reshape(batch * N_MEM, d).astype(BF16)

    ffn1_up, ffn1_down = _cast_bf16(ffn1_w_up), _cast_bf16(ffn1_w_down)
    ffn2_up, ffn2_down = _cast_bf16(ffn2_w_up), _cast_bf16(ffn2_w_down)
    w_in_b, mem_kv_b = _cast_bf16(w_in), _cast_bf16(mem_w_kv)
    w_conv_b, w_win_b = _cast_bf16(conv_w_out), _cast_bf16(win_w_o)
    w_mem_b, w_out_b = _cast_bf16(mem_w_o), _cast_bf16(w_out)

    h = x.reshape(t, d)
    hb = _cast_bf16(x.reshape(1, t, d)).reshape(t, d)
    for l in range(DEPTH):
        h, hb = _ffn(hb, h, ffn1_up, ffn1_down, l, row(ln1_g[l]), row(ln1_b[l]))

        glu = _proj_call(_glu_kernel, hb, w_in_b, l, [OFF_CONV, OFF_CONV + CONV_CH], CONV_CH,
                         [], [], CONV_CH, F32, "proj_glu")
        qkvm = _proj_call(_qkvm_kernel, hb, w_in_b, l, [OFF_QKV], QKVM_TN,
                          [cos_t, sin_lo, sin_hi], rope_specs, QKVM_WIDTH, BF16, "proj_qkvm")
        gates, hc = _gates_and_conv(hb, w_in_b, l, glu, conv_dw_w[l], row(conv_dw_b[l]),
                                    row(conv_ln_g[l]), row(conv_ln_b[l]), seq)
        mem_kv = _proj_call(_plain_kernel, mem_b, mem_kv_b, l, [0], MEM_KV_TN,
                            [], [], 2 * MEM_WIDTH, BF16, "proj_mem_kv")

        ow = _win_attention(qkvm, win_sink[l], batch, seq)
        om = _mem_attention(qkvm, mem_kv, batch, seq)

        h, hb = _mix_out(hc, ow.reshape(t, Q_WIDTH), om.reshape(t, MEM_WIDTH),
                         gates, h, w_conv_b, w_win_b, w_mem_b, w_out_b, l,
                         row(ln2_g[l]), row(ln2_b[l]))

        h, hb = _ffn(hb, h, ffn2_up, ffn2_down, l, row(ln3_g[l]), row(ln3_b[l]))
    return h.reshape(batch, seq, d)
```

```python
import functools

import jax
import jax.numpy as jnp
from jax import lax
from jax.experimental import pallas as pl
from jax.experimental.pallas import tpu as pltpu

D_MODEL = 2048
DEPTH = 2
N_MEM = 256
CONV_CH = 1024
CONV_WIDTH = 31
CONV_HALF = CONV_WIDTH // 2
HEAD_DIM = 128
N_Q_HEADS = 8
N_KV_HEADS = 2
GROUP = N_Q_HEADS // N_KV_HEADS
Q_WIDTH = N_Q_HEADS * HEAD_DIM
KV_WIDTH = N_KV_HEADS * HEAD_DIM
WINDOW = 128
BLOCK = 128
ROT_DIM = HEAD_DIM // 4
ROT_HALF = ROT_DIM // 2
ROPE_THETA = 500000.0
MEM_HEADS = 4
MEM_HEAD_DIM = 256
MEM_WIDTH = MEM_HEADS * MEM_HEAD_DIM
N_BRANCH = 3
D_FF = 5632
ALPHA = (2 * DEPTH) ** 0.25
LN_EPS = 1e-5
NEG_INF = -1e30

OFF_CONV = 0
OFF_QKV = 2 * CONV_CH
QKVM_WIDTH = Q_WIDTH + 2 * KV_WIDTH + MEM_WIDTH
OFF_GATE = OFF_QKV + QKVM_WIDTH

V7X_LANES = 128
V7X_SUBLANES = 8
V7X_VMEM_BYTES = 64 * 1024 * 1024
VMEM_REQUEST_CAP = V7X_VMEM_BYTES - 8 * 1024 * 1024

F32 = jnp.float32
BF16 = jnp.bfloat16


def _vmem_limit(block_bytes, temp_bytes):
    need = int((2 * block_bytes + temp_bytes) * 1.25)
    return min(max(need, 16 * 1024 * 1024), VMEM_REQUEST_CAP)


def _params(semantics, vmem_bytes):
    return pltpu.CompilerParams(dimension_semantics=semantics, vmem_limit_bytes=vmem_bytes)


def _layer_norm_rows(z, g, b):
    mu = jnp.mean(z, axis=-1, keepdims=True)
    zc = z - mu
    var = jnp.mean(zc * zc, axis=-1, keepdims=True)
    return zc * lax.rsqrt(var + LN_EPS) * g + b


def _dot(a, b):
    return jnp.dot(a, b, preferred_element_type=F32)


def _dot_nt(a, b):
    return lax.dot_general(a, b, (((1,), (1,)), ((), ())), preferred_element_type=F32)


CAST_BLOCK_BYTES = 8 * 1024 * 1024


def _cast_kernel(w_ref, o_ref):
    o_ref[...] = w_ref[...].astype(o_ref.dtype)


def _cast_bf16(w):
    depth, rows, cols = w.shape
    rb = rows
    while rb * cols * 4 > CAST_BLOCK_BYTES and rb % 2 == 0 and (rb // 2) % 16 == 0:
        rb //= 2
    return pl.pallas_call(
        _cast_kernel,
        grid=(depth, rows // rb),
        in_specs=[pl.BlockSpec((None, rb, cols), lambda l, i: (l, i, 0))],
        out_specs=pl.BlockSpec((None, rb, cols), lambda l, i: (l, i, 0)),
        out_shape=jax.ShapeDtypeStruct(w.shape, BF16),
        compiler_params=_params(("parallel", "parallel"), _vmem_limit(rb * cols * 6, 0)),
        name="cast_bf16",
    )(w)


FFN_TM = 1024
FFN_TF = 512
FFN_LN_ROWS = 256


def _ffn_kernel(xm_ref, x_hbm, wg_ref, wu_ref, wd_ref, g_ref, b_ref, o_hbm, ob_hbm,
                acc_ref, xy_ref, yb_ref, sems):
    i, j = pl.program_id(0), pl.program_id(1)
    last_i, last_j = pl.num_programs(0) - 1, pl.num_programs(1) - 1
    rows = pl.ds(pl.multiple_of(i * FFN_TM, FFN_TM), FFN_TM)
    x_in = pltpu.make_async_copy(x_hbm.at[rows], xy_ref, sems.at[0])
    y_out = pltpu.make_async_copy(xy_ref, o_hbm.at[rows], sems.at[1])
    yb_out = pltpu.make_async_copy(yb_ref, ob_hbm.at[rows], sems.at[2])

    @pl.when((i == 0) & (j == 0))
    def _():
        acc_ref[...] = jnp.zeros_like(acc_ref)

    @pl.when(j == 1)
    def _():
        @pl.when(i > 0)
        def _():
            y_out.wait()
            yb_out.wait()

        x_in.start()

    xb = xm_ref[...]
    gate = _dot(xb, wg_ref[...])
    up = _dot(xb, wu_ref[...])
    h = (gate * jax.nn.sigmoid(gate) * up).astype(BF16)
    acc_ref[...] += _dot(h, wd_ref[...])

    @pl.when(j == last_j)
    def _():
        x_in.wait()

        def norm(c, carry):
            r = pl.ds(pl.multiple_of(c * FFN_LN_ROWS, FFN_LN_ROWS), FFN_LN_ROWS)
            y = _layer_norm_rows(ALPHA * xy_ref[r, :] + 0.5 * acc_ref[r, :], g_ref[...], b_ref[...])
            xy_ref[r, :] = y
            yb_ref[r, :] = y.astype(yb_ref.dtype)
            acc_ref[r, :] = jnp.zeros((FFN_LN_ROWS, acc_ref.shape[1]), F32)
            return carry

        lax.fori_loop(0, FFN_TM // FFN_LN_ROWS, norm, 0)
        y_out.start()
        yb_out.start()

        @pl.when(i == last_i)
        def _():
            y_out.wait()
            yb_out.wait()


def _ffn(xm, x, w_up, w_down, layer, ln_g, ln_b):
    t, d = x.shape
    nf = D_FF // FFN_TF
    assert nf >= 3 and xm.dtype == BF16
    blocks = FFN_TM * d * 2 + 3 * d * FFN_TF * 2
    scratch = FFN_TM * d * (4 + 4 + 2)
    temps = scratch + 4 * FFN_TM * FFN_TF * 4 + 4 * FFN_LN_ROWS * d * 4
    hbm = lambda: pl.BlockSpec(memory_space=pl.ANY)
    return pl.pallas_call(
        _ffn_kernel,
        grid=(t // FFN_TM, nf),
        in_specs=[
            pl.BlockSpec((FFN_TM, d), lambda i, j: (i, 0)),
            hbm(),
            pl.BlockSpec((None, d, FFN_TF), lambda i, j: (layer, 0, j)),
            pl.BlockSpec((None, d, FFN_TF), lambda i, j: (layer, 0, nf + j)),
            pl.BlockSpec((None, FFN_TF, d), lambda i, j: (layer, j, 0)),
            pl.BlockSpec((1, d), lambda i, j: (0, 0)),
            pl.BlockSpec((1, d), lambda i, j: (0, 0)),
        ],
        out_specs=[hbm(), hbm()],
        out_shape=[jax.ShapeDtypeStruct((t, d), F32), jax.ShapeDtypeStruct((t, d), BF16)],
        scratch_shapes=[pltpu.VMEM((FFN_TM, d), F32), pltpu.VMEM((FFN_TM, d), F32),
                        pltpu.VMEM((FFN_TM, d), BF16), pltpu.SemaphoreType.DMA((3,))],
        compiler_params=_params(("arbitrary", "arbitrary"), _vmem_limit(blocks, temps)),
        name="ffn",
    )(xm, x, w_up, w_up, w_down, ln_g, ln_b)


PROJ_TM = 1024


def _glu_kernel(x_ref, wa_ref, wg_ref, o_ref):
    x = x_ref[...]
    o_ref[...] = _dot(x, wa_ref[...]) * jax.nn.sigmoid(_dot(x, wg_ref[...]))


def _rope(r, cos, sin_lo, sin_hi):
    n = r.shape[1] // HEAD_DIM
    tile = lambda tbl: jnp.concatenate([tbl] * n, axis=1) if n > 1 else tbl
    from_lower = pltpu.roll(r, ROT_HALF, axis=1)
    from_upper = pltpu.roll(r, r.shape[1] - ROT_HALF, axis=1)
    return r * tile(cos) + from_lower * tile(sin_lo) + from_upper * tile(sin_hi)


QKVM_TN = 512
KV_TILE = Q_WIDTH // QKVM_TN
GATE_TN = 1536
MEM_KV_TN = 1024


def _qkvm_kernel(x_ref, w_ref, cos_ref, slo_ref, shi_ref, o_ref):
    j = pl.program_id(1)

    @pl.when(j < KV_TILE)
    def _():
        r = _dot(x_ref[...], w_ref[...])
        o_ref[...] = _rope(r, cos_ref[...], slo_ref[...], shi_ref[...]).astype(o_ref.dtype)

    @pl.when(j == KV_TILE)
    def _():
        r = _dot(x_ref[...], w_ref[...])
        k = _rope(r[:, :KV_WIDTH], cos_ref[...], slo_ref[...], shi_ref[...])
        o_ref[...] = jnp.concatenate([k, r[:, KV_WIDTH:]], axis=1).astype(o_ref.dtype)

    @pl.when(j > KV_TILE)
    def _():
        o_ref[...] = _dot(x_ref[...], w_ref[...]).astype(o_ref.dtype)


def _plain_kernel(x_ref, w_ref, o_ref):
    o_ref[...] = _dot(x_ref[...], w_ref[...]).astype(o_ref.dtype)


def _proj_call(kernel, x, weight, layer, col_offsets, tn, extra, extra_specs, n_out, out_dtype, name):
    t, d = x.shape
    out_bytes = jnp.dtype(out_dtype).itemsize
    n_w = len(col_offsets)
    blocks = PROJ_TM * d * 2 + n_w * d * tn * 2 + PROJ_TM * tn * out_bytes
    temps = (2 + 2 * n_w) * PROJ_TM * tn * 4
    assert all(off % tn == 0 for off in col_offsets) and n_out % tn == 0
    w_specs = [pl.BlockSpec((None, d, tn), functools.partial(lambda i, j, blk: (layer, 0, blk + j), blk=off // tn))
               for off in col_offsets]
    return pl.pallas_call(
        kernel,
        grid=(t // PROJ_TM, n_out // tn),
        in_specs=[pl.BlockSpec((PROJ_TM, d), lambda i, j: (i, 0))] + w_specs + extra_specs,
        out_specs=pl.BlockSpec((PROJ_TM, tn), lambda i, j: (i, j)),
        out_shape=jax.ShapeDtypeStruct((t, n_out), out_dtype),
        compiler_params=_params(("parallel", "arbitrary"), _vmem_limit(blocks, temps)),
        name=name,
    )(x, *([weight] * n_w), *extra)


def _rope_tables(seq_len):
    pos = jnp.arange(seq_len, dtype=F32)
    inv_freq = ROPE_THETA ** (-jnp.arange(0, ROT_DIM, 2, dtype=F32) / ROT_DIM)
    ang = pos[:, None] * inv_freq[None, :]
    cos, sin = jnp.cos(ang), jnp.sin(ang)
    rest = HEAD_DIM - ROT_DIM
    zeros = lambda n: jnp.zeros((seq_len, n), F32)
    cos_t = jnp.concatenate([cos, cos, jnp.ones((seq_len, rest), F32)], axis=1)
    sin_lo = jnp.concatenate([zeros(ROT_HALF), sin, zeros(rest)], axis=1)
    sin_hi = jnp.concatenate([-sin, zeros(ROT_HALF), zeros(rest)], axis=1)
    return cos_t, sin_lo, sin_hi


CONV_TILE = 256
CONV_HALO = 16
CONV_WIN = CONV_TILE + 2 * CONV_HALO
CONV_ROWS = 64
CONV_TAP0 = CONV_HALO - CONV_HALF
CONV_SH_ROWS = CONV_TILE + (CONV_TAP0 + CONV_WIDTH - 1) // V7X_SUBLANES * V7X_SUBLANES


def _conv_window_dma(glu_hbm, win_ref, sems, chunk, slot, n_chunks, act):
    body = CONV_WIN - CONV_HALO
    total = glu_hbm.shape[0]

    @pl.when(chunk == 0)
    def _():
        act(pltpu.make_async_copy(glu_hbm.at[pl.ds(0, body)],
                                  win_ref.at[slot, pl.ds(CONV_HALO, body)], sems.at[slot]))

    @pl.when(chunk == n_chunks - 1)
    def _():
        act(pltpu.make_async_copy(glu_hbm.at[pl.ds(total - body, body)],
                                  win_ref.at[slot, pl.ds(0, body)], sems.at[slot]))

    @pl.when((chunk > 0) & (chunk < n_chunks - 1))
    def _():
        start = pl.multiple_of(chunk * CONV_TILE - CONV_HALO, CONV_HALO)
        act(pltpu.make_async_copy(glu_hbm.at[pl.ds(start, CONV_WIN)], win_ref.at[slot], sems.at[slot]))


def _gate_conv_kernel(chunks_per_seq, x_ref, w_ref, glu_hbm, cw_ref, cb_ref, g_ref, b_ref,
                      gate_ref, hc_ref, win_ref, sh_ref, cv_ref, sems):
    i, j = pl.program_id(0), pl.program_id(1)
    n_chunks = pl.num_programs(0) * pl.num_programs(1)
    chunk = i * pl.num_programs(1) + j
    slot = chunk % 2
    c = cv_ref.shape[1]

    @pl.when(chunk == 0)
    def _():
        win_ref[0, pl.ds(0, CONV_HALO), :] = jnp.zeros((CONV_HALO, c), F32)
        _conv_window_dma(glu_hbm, win_ref, sems, chunk, slot, n_chunks, lambda cp: cp.start())

    _conv_window_dma(glu_hbm, win_ref, sems, chunk, slot, n_chunks, lambda cp: cp.wait())

    @pl.when(chunk == n_chunks - 1)
    def _():
        win_ref[slot, pl.ds(CONV_WIN - CONV_HALO, CONV_HALO), :] = jnp.zeros((CONV_HALO, c), F32)

    @pl.when(chunk + 1 < n_chunks)
    def _():
        _conv_window_dma(glu_hbm, win_ref, sems, chunk + 1, 1 - slot, n_chunks, lambda cp: cp.start())

    pos = chunk % chunks_per_seq
    row = lax.broadcasted_iota(jnp.int32, (CONV_WIN, V7X_LANES), 0)
    keep = ((row >= jnp.where(pos == 0, CONV_HALO, 0))
            & (row < jnp.where(pos == chunks_per_seq - 1, CONV_WIN - CONV_HALO, CONV_WIN)))

    for ct in range(c // V7X_LANES):
        lanes = slice(ct * V7X_LANES, (ct + 1) * V7X_LANES)
        win = jnp.where(keep, win_ref[slot, :, lanes], 0.0)
        sh = sh_ref.at[ct % 2]
        for p in range(V7X_SUBLANES):
            sh[p] = win[p:p + CONV_SH_ROWS]
        for r0 in range(0, CONV_TILE, CONV_ROWS):
            acc = jnp.broadcast_to(cb_ref[:, lanes], (CONV_ROWS, V7X_LANES))
            for k in range(CONV_WIDTH):
                off = CONV_TAP0 + k
                rows = pl.ds(r0 + off // V7X_SUBLANES * V7X_SUBLANES, CONV_ROWS)
                acc = acc + sh[off % V7X_SUBLANES, rows, :] * cw_ref[k:k + 1, lanes]
            cv_ref[pl.ds(r0, CONV_ROWS), lanes] = acc
    y = _layer_norm_rows(cv_ref[...], g_ref[...], b_ref[...])
    hc_ref[pl.ds(pl.multiple_of(j * CONV_TILE, CONV_TILE), CONV_TILE), :] = (
        y * jax.nn.sigmoid(y)).astype(hc_ref.dtype)

    gate_ref[...] = jax.nn.sigmoid(_dot(x_ref[...], w_ref[...])).astype(gate_ref.dtype)


def _gates_and_conv(x, w_in, layer, glu, dw_w, dw_b, ln_g, ln_b, seq):
    t, d = x.shape
    c = glu.shape[1]
    n_gate = N_BRANCH * d
    steps = n_gate // GATE_TN
    assert PROJ_TM == steps * CONV_TILE and seq % CONV_TILE == 0 and OFF_GATE % GATE_TN == 0
    assert CONV_SH_ROWS + V7X_SUBLANES - 1 <= CONV_WIN
    blocks = PROJ_TM * d * 2 + d * GATE_TN * 2 + PROJ_TM * GATE_TN * 2 + PROJ_TM * c * 2
    scratch = (2 * CONV_WIN * c + 2 * V7X_SUBLANES * CONV_SH_ROWS * V7X_LANES + CONV_TILE * c) * 4
    temps = scratch + 3 * PROJ_TM * GATE_TN * 4
    small = lambda r: pl.BlockSpec((r, c), lambda i, j: (0, 0))
    return pl.pallas_call(
        functools.partial(_gate_conv_kernel, seq // CONV_TILE),
        grid=(t // PROJ_TM, steps),
        in_specs=[
            pl.BlockSpec((PROJ_TM, d), lambda i, j: (i, 0)),
            pl.BlockSpec((None, d, GATE_TN), lambda i, j: (layer, 0, OFF_GATE // GATE_TN + j)),
            pl.BlockSpec(memory_space=pl.ANY),
            small(CONV_WIDTH), small(1), small(1), small(1),
        ],
        out_specs=[pl.BlockSpec((PROJ_TM, GATE_TN), lambda i, j: (i, j)),
                   pl.BlockSpec((PROJ_TM, c), lambda i, j: (i, 0))],
        out_shape=[jax.ShapeDtypeStruct((t, n_gate), BF16), jax.ShapeDtypeStruct((t, c), BF16)],
        scratch_shapes=[pltpu.VMEM((2, CONV_WIN, c), F32),
                        pltpu.VMEM((2, V7X_SUBLANES, CONV_SH_ROWS, V7X_LANES), F32),
                        pltpu.VMEM((CONV_TILE, c), F32),
                        pltpu.SemaphoreType.DMA((2,))],
        compiler_params=_params(("arbitrary", "arbitrary"), _vmem_limit(blocks, temps)),
        name="gates_conv",
    )(x, w_in, glu, dw_w, dw_b, ln_g, ln_b)


WIN_KEYS = 3 * BLOCK
WIN_BLOCKS = 4


def _masked_softmax_with_sink(sc, sink):
    m = jnp.maximum(jnp.max(sc, axis=1, keepdims=True), sink)
    e = jnp.exp(sc - m)
    denom = jnp.sum(e, axis=1, keepdims=True) + jnp.exp(sink - m)
    return e * (1.0 / denom)


def _win_attn_kernel(sink_ref, q_ref, kv_ref, o_ref):
    s = q_ref.shape[0]
    key_minus_query = (lax.broadcasted_iota(jnp.int32, (BLOCK, WIN_KEYS), 1)
                       - lax.broadcasted_iota(jnp.int32, (BLOCK, WIN_KEYS), 0))
    scale = HEAD_DIM ** -0.5

    def blocks(it, carry):
        chains = []
        for b in range(WIN_BLOCKS):
            q0 = pl.multiple_of((it * WIN_BLOCKS + b) * BLOCK, BLOCK)
            k0 = pl.multiple_of(jnp.clip(q0 - BLOCK, 0, s - WIN_KEYS), BLOCK)
            valid = jnp.abs(key_minus_query + (k0 - q0)) <= WINDOW
            for kvh in range(N_KV_HEADS):
                heads = [kvh * GROUP + g for g in range(GROUP)]
                qg = jnp.concatenate(
                    [q_ref[pl.ds(q0, BLOCK), h * HEAD_DIM:(h + 1) * HEAD_DIM] for h in heads], axis=0)
                k = kv_ref[pl.ds(k0, WIN_KEYS), kvh * HEAD_DIM:(kvh + 1) * HEAD_DIM]
                v = kv_ref[pl.ds(k0, WIN_KEYS),
                           KV_WIDTH + kvh * HEAD_DIM:KV_WIDTH + (kvh + 1) * HEAD_DIM]
                chains.append((q0, heads, valid, _dot_nt(qg, k), v))
        outs = []
        for _, heads, valid, sc_all, v in chains:
            p = [_masked_softmax_with_sink(
                     jnp.where(valid, sc_all[g * BLOCK:(g + 1) * BLOCK] * scale, NEG_INF), sink_ref[h])
                 for g, h in enumerate(heads)]
            outs.append(_dot(jnp.concatenate(p, axis=0).astype(BF16), v))
        for (q0, heads, _, _, _), o in zip(chains, outs):
            for g, h in enumerate(heads):
                o_ref[pl.ds(q0, BLOCK), h * HEAD_DIM:(h + 1) * HEAD_DIM] = (
                    o[g * BLOCK:(g + 1) * BLOCK].astype(o_ref.dtype))
        return carry

    lax.fori_loop(0, s // (BLOCK * WIN_BLOCKS), blocks, 0)


def _win_attention(qkvm, sink, batch, seq):
    x = qkvm.reshape(batch, seq, QKVM_WIDTH)
    kv_block = Q_WIDTH // (2 * KV_WIDTH)
    blocks = seq * (2 * Q_WIDTH + 2 * KV_WIDTH) * 2
    temps = 12 * WIN_BLOCKS * N_KV_HEADS * GROUP * BLOCK * WIN_KEYS * 4
    return pl.pallas_call(
        _win_attn_kernel,
        grid=(batch,),
        in_specs=[
            pl.BlockSpec(memory_space=pltpu.SMEM),
            pl.BlockSpec((None, seq, Q_WIDTH), lambda i: (i, 0, 0)),
            pl.BlockSpec((None, seq, 2 * KV_WIDTH), lambda i: (i, 0, kv_block)),
        ],
        out_specs=pl.BlockSpec((None, seq, Q_WIDTH), lambda i: (i, 0, 0)),
        out_shape=jax.ShapeDtypeStruct((batch, seq, Q_WIDTH), BF16),
        compiler_params=_params(("parallel",), _vmem_limit(blocks, temps)),
        name="win_attn",
    )(sink, x, x)


MEM_ROWS = 512


def _mem_attn_kernel(q_ref, kv_ref, o_ref):
    scale = MEM_HEAD_DIM ** -0.5

    def chunk(i, carry):
        r0 = pl.multiple_of(i * MEM_ROWS, MEM_ROWS)
        cols = [slice(h * MEM_HEAD_DIM, (h + 1) * MEM_HEAD_DIM) for h in range(MEM_HEADS)]
        scores = [_dot_nt(q_ref[pl.ds(r0, MEM_ROWS), c], kv_ref[:, c]) * scale for c in cols]
        outs = []
        for h, sc in enumerate(scores):
            e = jnp.exp(sc - jnp.max(sc, axis=1, keepdims=True))
            p = (e * (1.0 / jnp.sum(e, axis=1, keepdims=True))).astype(BF16)
            vcols = slice(MEM_WIDTH + h * MEM_HEAD_DIM, MEM_WIDTH + (h + 1) * MEM_HEAD_DIM)
            outs.append(_dot(p, kv_ref[:, vcols]))
        for c, o in zip(cols, outs):
            o_ref[pl.ds(r0, MEM_ROWS), c] = o.astype(o_ref.dtype)
        return carry

    lax.fori_loop(0, q_ref.shape[0] // MEM_ROWS, chunk, 0)


def _mem_attention(qkvm, mem_kv, batch, seq):
    x = qkvm.reshape(batch, seq, QKVM_WIDTH)
    kv = mem_kv.reshape(batch, N_MEM, 2 * MEM_WIDTH)
    half = MEM_WIDTH // 2
    first = (Q_WIDTH + 2 * KV_WIDTH) // half
    blocks = seq * 2 * MEM_WIDTH * 2 + N_MEM * 2 * MEM_WIDTH * 2
    temps = 12 * MEM_HEADS * MEM_ROWS * N_MEM * 4

    def body(qa_ref, qb_ref, kv_ref, o_ref, q_ref):
        q_ref[:, :half] = qa_ref[...]
        q_ref[:, half:] = qb_ref[...]
        _mem_attn_kernel(q_ref, kv_ref, o_ref)

    return pl.pallas_call(
        body,
        grid=(batch,),
        in_specs=[
            pl.BlockSpec((None, seq, half), lambda i: (i, 0, first)),
            pl.BlockSpec((None, seq, half), lambda i: (i, 0, first + 1)),
            pl.BlockSpec((None, N_MEM, 2 * MEM_WIDTH), lambda i: (i, 0, 0)),
        ],
        out_specs=pl.BlockSpec((None, seq, MEM_WIDTH), lambda i: (i, 0, 0)),
        out_shape=jax.ShapeDtypeStruct((batch, seq, MEM_WIDTH), BF16),
        scratch_shapes=[pltpu.VMEM((seq, MEM_WIDTH), BF16)],
        compiler_params=_params(("parallel",), _vmem_limit(blocks, temps + seq * MEM_WIDTH * 2)),
        name="mem_attn",
    )(x, x, kv)


MIX_TM = 256


SIDE_UP_ROWS = 32
SIDE_DOWN_ROWS = 176


def _side_cast(src_hbm, dst_hbm, in_ref, out_ref, sems, sem0, layer, step, rows):
    n_chunks = dst_hbm.shape[0] // rows

    def copies(s):
        slot = s % 2
        r = pl.ds(pl.multiple_of(s * rows, rows), rows)
        return (pltpu.make_async_copy(src_hbm.at[layer, r], in_ref.at[slot], sems.at[sem0 + slot]),
                pltpu.make_async_copy(out_ref.at[slot], dst_hbm.at[r], sems.at[sem0 + 2 + slot]))

    @pl.when(step < n_chunks)
    def _():
        fetch, flush = copies(step)

        @pl.when(step == 0)
        def _():
            fetch.start()

        fetch.wait()

        @pl.when(step + 1 < n_chunks)
        def _():
            copies(step + 1)[0].start()

        @pl.when(step >= 2)
        def _():
            copies(step - 2)[1].wait()

        out_ref[step % 2] = in_ref[step % 2].astype(out_ref.dtype)
        flush.start()

        @pl.when(step == n_chunks - 1)
        def _():
            flush.wait()
            copies(step - 1)[1].wait()


def _mix_out_kernel(layer, hc_ref, ow_ref, om_ref, gate_ref, h_ref, wc_ref, ww_ref, wm_ref, wo_ref,
                    g_ref, b_ref, up_hbm, down_hbm, o_ref, ob_ref, upb_hbm, downb_hbm,
                    up_in, up_out, down_in, down_out, sems):
    step = pl.program_id(0)
    _side_cast(up_hbm, upb_hbm, up_in, up_out, sems, 0, layer, step, SIDE_UP_ROWS)
    _side_cast(down_hbm, downb_hbm, down_in, down_out, sems, 4, layer, step, SIDE_DOWN_ROWS)

    d = h_ref.shape[1]
    merged = gate_ref[:, 0:d].astype(F32) * _dot(hc_ref[...], wc_ref[...])
    merged += gate_ref[:, d:2 * d].astype(F32) * _dot(ow_ref[...], ww_ref[...])
    merged += gate_ref[:, 2 * d:3 * d].astype(F32) * _dot(om_ref[...], wm_ref[...])
    y = _dot(merged.astype(BF16), wo_ref[...])
    y = _layer_norm_rows(ALPHA * h_ref[...] + y, g_ref[...], b_ref[...])
    o_ref[...] = y
    ob_ref[...] = y.astype(ob_ref.dtype)


def _mix_out(hc, ow, om, gates, h, w_conv, w_win, w_mem, w_out, layer, ln_g, ln_b, next_up, next_down):
    t, d = h.shape
    steps = t // MIX_TM
    up_shape, down_shape = next_up.shape[1:], next_down.shape[1:]
    assert up_shape[0] == steps * SIDE_UP_ROWS and down_shape[0] % SIDE_DOWN_ROWS == 0
    assert 2 <= down_shape[0] // SIDE_DOWN_ROWS <= steps
    rows = lambda width: pl.BlockSpec((MIX_TM, width), lambda i: (i, 0))
    resident = lambda k: pl.BlockSpec((None, k, d), lambda i: (layer, 0, 0), pipeline_mode=pl.Buffered(1))
    hbm = lambda: pl.BlockSpec(memory_space=pl.ANY)
    weight_bytes = (CONV_CH + Q_WIDTH + MEM_WIDTH + d) * d * 2
    blocks = MIX_TM * ((CONV_CH + Q_WIDTH + MEM_WIDTH) * 2 + N_BRANCH * d * 2 + 2 * d * 4 + d * 2)
    side = 2 * (SIDE_UP_ROWS * up_shape[1] + SIDE_DOWN_ROWS * down_shape[1]) * (4 + 2)
    temps = weight_bytes + 6 * MIX_TM * d * 4 + side
    return pl.pallas_call(
        functools.partial(_mix_out_kernel, layer),
        grid=(steps,),
        in_specs=[
            rows(CONV_CH), rows(Q_WIDTH), rows(MEM_WIDTH), rows(N_BRANCH * d), rows(d),
            resident(CONV_CH), resident(Q_WIDTH), resident(MEM_WIDTH), resident(d),
            pl.BlockSpec((1, d), lambda i: (0, 0)), pl.BlockSpec((1, d), lambda i: (0, 0)),
            hbm(), hbm(),
        ],
        out_specs=[rows(d), rows(d), hbm(), hbm()],
        out_shape=[jax.ShapeDtypeStruct((t, d), F32), jax.ShapeDtypeStruct((t, d), BF16),
                   jax.ShapeDtypeStruct(up_shape, BF16), jax.ShapeDtypeStruct(down_shape, BF16)],
        scratch_shapes=[pltpu.VMEM((2, SIDE_UP_ROWS, up_shape[1]), F32),
                        pltpu.VMEM((2, SIDE_UP_ROWS, up_shape[1]), BF16),
                        pltpu.VMEM((2, SIDE_DOWN_ROWS, down_shape[1]), F32),
                        pltpu.VMEM((2, SIDE_DOWN_ROWS, down_shape[1]), BF16),
                        pltpu.SemaphoreType.DMA((8,))],
        compiler_params=_params(("arbitrary",), _vmem_limit(blocks, temps)),
        name="mix_out",
    )(hc, ow, om, gates, h, w_conv, w_win, w_mem, w_out, ln_g, ln_b, next_up, next_down)


def kernel(x, mem, ln1_g, ln1_b, ffn1_w_up, ffn1_w_down, w_in, conv_dw_w, conv_dw_b, conv_ln_g,
           conv_ln_b, conv_w_out, win_w_o, win_sink, mem_w_kv, mem_w_o, w_out, ln2_g, ln2_b,
           ffn2_w_up, ffn2_w_down, ln3_g, ln3_b):
    batch, seq, d = x.shape
    t = batch * seq
    cos_t, sin_lo, sin_hi = _rope_tables(seq)
    rope_specs = [pl.BlockSpec((PROJ_TM, HEAD_DIM), lambda i, j: (i % (seq // PROJ_TM), 0))] * 3
    row = lambda v: v.reshape(1, -1)
    mem_b = mem.reshape(batch * N_MEM, d).astype(BF16)

    ffn1_up, ffn1_down = _cast_bf16(ffn1_w_up), _cast_bf16(ffn1_w_down)
    w_in_b, mem_kv_b = _cast_bf16(w_in), _cast_bf16(mem_w_kv)
    w_conv_b, w_win_b = _cast_bf16(conv_w_out), _cast_bf16(win_w_o)
    w_mem_b, w_out_b = _cast_bf16(mem_w_o), _cast_bf16(w_out)

    h = x.reshape(t, d)
    hb = _cast_bf16(x.reshape(1, t, d)).reshape(t, d)
    for l in range(DEPTH):
        h, hb = _ffn(hb, h, ffn1_up, ffn1_down, l, row(ln1_g[l]), row(ln1_b[l]))

        glu = _proj_call(_glu_kernel, hb, w_in_b, l, [OFF_CONV, OFF_CONV + CONV_CH], CONV_CH,
                         [], [], CONV_CH, F32, "proj_glu")
        qkvm = _proj_call(_qkvm_kernel, hb, w_in_b, l, [OFF_QKV], QKVM_TN,
                          [cos_t, sin_lo, sin_hi], rope_specs, QKVM_WIDTH, BF16, "proj_qkvm")
        gates, hc = _gates_and_conv(hb, w_in_b, l, glu, conv_dw_w[l], row(conv_dw_b[l]),
                                    row(conv_ln_g[l]), row(conv_ln_b[l]), seq)
        mem_kv = _proj_call(_plain_kernel, mem_b, mem_kv_b, l, [0], MEM_KV_TN,
                            [], [], 2 * MEM_WIDTH, BF16, "proj_mem_kv")

        ow = _win_attention(qkvm, win_sink[l], batch, seq)
        om = _mem_attention(qkvm, mem_kv, batch, seq)

        h, hb, ffn2_up, ffn2_down = _mix_out(
            hc, ow.reshape(t, Q_WIDTH), om.reshape(t, MEM_WIDTH), gates, h, w_conv_b, w_win_b,
            w_mem_b, w_out_b, l, row(ln2_g[l]), row(ln2_b[l]), ffn2_w_up, ffn2_w_down)

        h, hb = _ffn(hb, h, ffn2_up[None], ffn2_down[None], 0, row(ln3_g[l]), row(ln3_b[l]))
    return h.reshape(batch, seq, d)
```

```python
import functools

import jax
import jax.numpy as jnp
from jax import lax
from jax.experimental import pallas as pl
from jax.experimental.pallas import tpu as pltpu

D_MODEL = 2048
DEPTH = 2
N_MEM = 256
CONV_CH = 1024
CONV_WIDTH = 31
CONV_HALF = CONV_WIDTH // 2
HEAD_DIM = 128
N_Q_HEADS = 8
N_KV_HEADS = 2
GROUP = N_Q_HEADS // N_KV_HEADS
Q_WIDTH = N_Q_HEADS * HEAD_DIM
KV_WIDTH = N_KV_HEADS * HEAD_DIM
WINDOW = 128
BLOCK = 128
ROT_DIM = HEAD_DIM // 4
ROT_HALF = ROT_DIM // 2
ROPE_THETA = 500000.0
MEM_HEADS = 4
MEM_HEAD_DIM = 256
MEM_WIDTH = MEM_HEADS * MEM_HEAD_DIM
N_BRANCH = 3
D_FF = 5632
ALPHA = (2 * DEPTH) ** 0.25
LN_EPS = 1e-5
NEG_INF = -1e30

OFF_CONV = 0
OFF_QKV = 2 * CONV_CH
QKVM_WIDTH = Q_WIDTH + 2 * KV_WIDTH + MEM_WIDTH
OFF_GATE = OFF_QKV + QKVM_WIDTH

V7X_LANES = 128
V7X_SUBLANES = 8
V7X_VMEM_BYTES = 64 * 1024 * 1024
VMEM_REQUEST_CAP = V7X_VMEM_BYTES - 8 * 1024 * 1024

F32 = jnp.float32
BF16 = jnp.bfloat16


def _vmem_limit(block_bytes, temp_bytes):
    need = int((2 * block_bytes + temp_bytes) * 1.25)
    return min(max(need, 16 * 1024 * 1024), VMEM_REQUEST_CAP)


def _params(semantics, vmem_bytes):
    return pltpu.CompilerParams(dimension_semantics=semantics, vmem_limit_bytes=vmem_bytes)


def _layer_norm_rows(z, g, b):
    mu = jnp.mean(z, axis=-1, keepdims=True)
    zc = z - mu
    var = jnp.mean(zc * zc, axis=-1, keepdims=True)
    return zc * lax.rsqrt(var + LN_EPS) * g + b


def _dot(a, b):
    return jnp.dot(a, b, preferred_element_type=F32)


def _dot_nt(a, b):
    return lax.dot_general(a, b, (((1,), (1,)), ((), ())), preferred_element_type=F32)


CAST_BLOCK_BYTES = 8 * 1024 * 1024


def _cast_kernel(w_ref, o_ref):
    o_ref[...] = w_ref[...].astype(o_ref.dtype)


def _cast_bf16(w, depth=None):
    depth = w.shape[0] if depth is None else depth
    _, rows, cols = w.shape
    rb = rows
    while rb * cols * 4 > CAST_BLOCK_BYTES and rb % 2 == 0 and (rb // 2) % 16 == 0:
        rb //= 2
    return pl.pallas_call(
        _cast_kernel,
        grid=(depth, rows // rb),
        in_specs=[pl.BlockSpec((None, rb, cols), lambda l, i: (l, i, 0))],
        out_specs=pl.BlockSpec((None, rb, cols), lambda l, i: (l, i, 0)),
        out_shape=jax.ShapeDtypeStruct((depth, rows, cols), BF16),
        compiler_params=_params(("parallel", "parallel"), _vmem_limit(rb * cols * 6, 0)),
        name="cast_bf16",
    )(w)


FFN_TM = 1024
FFN_TF = 512
FFN_LN_ROWS = 256


def _ffn_kernel(xm_ref, x_hbm, wg_ref, wu_ref, wd_ref, g_ref, b_ref, o_hbm, ob_hbm,
                acc_ref, xy_ref, yb_ref, sems):
    i, j = pl.program_id(0), pl.program_id(1)
    last_i, last_j = pl.num_programs(0) - 1, pl.num_programs(1) - 1
    rows = pl.ds(pl.multiple_of(i * FFN_TM, FFN_TM), FFN_TM)
    x_in = pltpu.make_async_copy(x_hbm.at[rows], xy_ref, sems.at[0])
    y_out = pltpu.make_async_copy(xy_ref, o_hbm.at[rows], sems.at[1])
    yb_out = pltpu.make_async_copy(yb_ref, ob_hbm.at[rows], sems.at[2])

    @pl.when((i == 0) & (j == 0))
    def _():
        acc_ref[...] = jnp.zeros_like(acc_ref)

    @pl.when(j == 1)
    def _():
        @pl.when(i > 0)
        def _():
            y_out.wait()
            yb_out.wait()

        x_in.start()

    xb = xm_ref[...]
    gate = _dot(xb, wg_ref[...])
    up = _dot(xb, wu_ref[...])
    h = (gate * jax.nn.sigmoid(gate) * up).astype(BF16)
    acc_ref[...] += _dot(h, wd_ref[...])

    @pl.when(j == last_j)
    def _():
        x_in.wait()

        def norm(c, carry):
            r = pl.ds(pl.multiple_of(c * FFN_LN_ROWS, FFN_LN_ROWS), FFN_LN_ROWS)
            y = _layer_norm_rows(ALPHA * xy_ref[r, :] + 0.5 * acc_ref[r, :], g_ref[...], b_ref[...])
            xy_ref[r, :] = y
            yb_ref[r, :] = y.astype(yb_ref.dtype)
            acc_ref[r, :] = jnp.zeros((FFN_LN_ROWS, acc_ref.shape[1]), F32)
            return carry

        lax.fori_loop(0, FFN_TM // FFN_LN_ROWS, norm, 0)
        y_out.start()
        yb_out.start()

        @pl.when(i == last_i)
        def _():
            y_out.wait()
            yb_out.wait()


def _ffn(xm, x, w_up, w_down, layer, ln_g, ln_b):
    t, d = x.shape
    nf = D_FF // FFN_TF
    assert nf >= 3 and xm.dtype == BF16
    blocks = FFN_TM * d * 2 + 3 * d * FFN_TF * 2
    scratch = FFN_TM * d * (4 + 4 + 2)
    temps = scratch + 4 * FFN_TM * FFN_TF * 4 + 4 * FFN_LN_ROWS * d * 4
    hbm = lambda: pl.BlockSpec(memory_space=pl.ANY)
    return pl.pallas_call(
        _ffn_kernel,
        grid=(t // FFN_TM, nf),
        in_specs=[
            pl.BlockSpec((FFN_TM, d), lambda i, j: (i, 0)),
            hbm(),
            pl.BlockSpec((None, d, FFN_TF), lambda i, j: (layer, 0, j)),
            pl.BlockSpec((None, d, FFN_TF), lambda i, j: (layer, 0, nf + j)),
            pl.BlockSpec((None, FFN_TF, d), lambda i, j: (layer, j, 0)),
            pl.BlockSpec((1, d), lambda i, j: (0, 0)),
            pl.BlockSpec((1, d), lambda i, j: (0, 0)),
        ],
        out_specs=[hbm(), hbm()],
        out_shape=[jax.ShapeDtypeStruct((t, d), F32), jax.ShapeDtypeStruct((t, d), BF16)],
        scratch_shapes=[pltpu.VMEM((FFN_TM, d), F32), pltpu.VMEM((FFN_TM, d), F32),
                        pltpu.VMEM((FFN_TM, d), BF16), pltpu.SemaphoreType.DMA((3,))],
        compiler_params=_params(("arbitrary", "arbitrary"), _vmem_limit(blocks, temps)),
        name="ffn",
    )(xm, x, w_up, w_up, w_down, ln_g, ln_b)


PROJ_TM = 1024


def _glu_kernel(x_ref, wa_ref, wg_ref, o_ref):
    x = x_ref[...]
    o_ref[...] = _dot(x, wa_ref[...]) * jax.nn.sigmoid(_dot(x, wg_ref[...]))


def _rope(r, cos, sin_lo, sin_hi):
    n = r.shape[1] // HEAD_DIM
    tile = lambda tbl: jnp.concatenate([tbl] * n, axis=1) if n > 1 else tbl
    from_lower = pltpu.roll(r, ROT_HALF, axis=1)
    from_upper = pltpu.roll(r, r.shape[1] - ROT_HALF, axis=1)
    return r * tile(cos) + from_lower * tile(sin_lo) + from_upper * tile(sin_hi)


QKVM_TN = 512
KV_TILE = Q_WIDTH // QKVM_TN
GATE_TN = 1536
MEM_KV_TN = 1024


def _qkvm_kernel(x_ref, w_ref, cos_ref, slo_ref, shi_ref, o_ref):
    j = pl.program_id(1)

    @pl.when(j < KV_TILE)
    def _():
        r = _dot(x_ref[...], w_ref[...])
        o_ref[...] = _rope(r, cos_ref[...], slo_ref[...], shi_ref[...]).astype(o_ref.dtype)

    @pl.when(j == KV_TILE)
    def _():
        r = _dot(x_ref[...], w_ref[...])
        k = _rope(r[:, :KV_WIDTH], cos_ref[...], slo_ref[...], shi_ref[...])
        o_ref[...] = jnp.concatenate([k, r[:, KV_WIDTH:]], axis=1).astype(o_ref.dtype)

    @pl.when(j > KV_TILE)
    def _():
        o_ref[...] = _dot(x_ref[...], w_ref[...]).astype(o_ref.dtype)


def _plain_kernel(x_ref, w_ref, o_ref):
    o_ref[...] = _dot(x_ref[...], w_ref[...]).astype(o_ref.dtype)


def _proj_call(kernel, x, weight, layer, col_offsets, tn, extra, extra_specs, n_out, out_dtype, name):
    t, d = x.shape
    out_bytes = jnp.dtype(out_dtype).itemsize
    n_w = len(col_offsets)
    blocks = PROJ_TM * d * 2 + n_w * d * tn * 2 + PROJ_TM * tn * out_bytes
    temps = (2 + 2 * n_w) * PROJ_TM * tn * 4
    assert all(off % tn == 0 for off in col_offsets) and n_out % tn == 0
    w_specs = [pl.BlockSpec((None, d, tn), functools.partial(lambda i, j, blk: (layer, 0, blk + j), blk=off // tn))
               for off in col_offsets]
    return pl.pallas_call(
        kernel,
        grid=(t // PROJ_TM, n_out // tn),
        in_specs=[pl.BlockSpec((PROJ_TM, d), lambda i, j: (i, 0))] + w_specs + extra_specs,
        out_specs=pl.BlockSpec((PROJ_TM, tn), lambda i, j: (i, j)),
        out_shape=jax.ShapeDtypeStruct((t, n_out), out_dtype),
        compiler_params=_params(("parallel", "arbitrary"), _vmem_limit(blocks, temps)),
        name=name,
    )(x, *([weight] * n_w), *extra)


def _rope_tables(seq_len):
    pos = jnp.arange(seq_len, dtype=F32)
    inv_freq = ROPE_THETA ** (-jnp.arange(0, ROT_DIM, 2, dtype=F32) / ROT_DIM)
    ang = pos[:, None] * inv_freq[None, :]
    cos, sin = jnp.cos(ang), jnp.sin(ang)
    rest = HEAD_DIM - ROT_DIM
    zeros = lambda n: jnp.zeros((seq_len, n), F32)
    cos_t = jnp.concatenate([cos, cos, jnp.ones((seq_len, rest), F32)], axis=1)
    sin_lo = jnp.concatenate([zeros(ROT_HALF), sin, zeros(rest)], axis=1)
    sin_hi = jnp.concatenate([-sin, zeros(ROT_HALF), zeros(rest)], axis=1)
    return cos_t, sin_lo, sin_hi


CONV_TILE = 256
CONV_HALO = 16
CONV_WIN = CONV_TILE + 2 * CONV_HALO
CONV_ROWS = 64
CONV_TAP0 = CONV_HALO - CONV_HALF
CONV_SH_ROWS = CONV_TILE + (CONV_TAP0 + CONV_WIDTH - 1) // V7X_SUBLANES * V7X_SUBLANES


def _conv_window_dma(glu_hbm, win_ref, sems, chunk, slot, n_chunks, act):
    body = CONV_WIN - CONV_HALO
    total = glu_hbm.shape[0]

    @pl.when(chunk == 0)
    def _():
        act(pltpu.make_async_copy(glu_hbm.at[pl.ds(0, body)],
                                  win_ref.at[slot, pl.ds(CONV_HALO, body)], sems.at[slot]))

    @pl.when(chunk == n_chunks - 1)
    def _():
        act(pltpu.make_async_copy(glu_hbm.at[pl.ds(total - body, body)],
                                  win_ref.at[slot, pl.ds(0, body)], sems.at[slot]))

    @pl.when((chunk > 0) & (chunk < n_chunks - 1))
    def _():
        start = pl.multiple_of(chunk * CONV_TILE - CONV_HALO, CONV_HALO)
        act(pltpu.make_async_copy(glu_hbm.at[pl.ds(start, CONV_WIN)], win_ref.at[slot], sems.at[slot]))


def _gate_conv_kernel(chunks_per_seq, side_layer, x_ref, w_ref, glu_hbm, cw_ref, cb_ref, g_ref, b_ref,
                      *rest):
    i, j = pl.program_id(0), pl.program_id(1)
    n_chunks = pl.num_programs(0) * pl.num_programs(1)
    chunk = i * pl.num_programs(1) + j
    if side_layer is None:
        gate_ref, hc_ref, win_ref, sh_ref, cv_ref, sems = rest
    else:
        (up_hbm, down_hbm, gate_ref, hc_ref, upb_hbm, downb_hbm, win_ref, sh_ref, cv_ref, sems,
         up_in, up_out, down_in, down_out, side_sems) = rest
        _side_cast(up_hbm, upb_hbm, up_in, up_out, side_sems, 0, side_layer, chunk, SIDE_UP_ROWS)
        _side_cast(down_hbm, downb_hbm, down_in, down_out, side_sems, 4, side_layer, chunk, SIDE_DOWN_ROWS)
    slot = chunk % 2
    c = cv_ref.shape[1]

    @pl.when(chunk == 0)
    def _():
        win_ref[0, pl.ds(0, CONV_HALO), :] = jnp.zeros((CONV_HALO, c), F32)
        _conv_window_dma(glu_hbm, win_ref, sems, chunk, slot, n_chunks, lambda cp: cp.start())

    _conv_window_dma(glu_hbm, win_ref, sems, chunk, slot, n_chunks, lambda cp: cp.wait())

    @pl.when(chunk == n_chunks - 1)
    def _():
        win_ref[slot, pl.ds(CONV_WIN - CONV_HALO, CONV_HALO), :] = jnp.zeros((CONV_HALO, c), F32)

    @pl.when(chunk + 1 < n_chunks)
    def _():
        _conv_window_dma(glu_hbm, win_ref, sems, chunk + 1, 1 - slot, n_chunks, lambda cp: cp.start())

    pos = chunk % chunks_per_seq
    row = lax.broadcasted_iota(jnp.int32, (CONV_WIN, V7X_LANES), 0)
    keep = ((row >= jnp.where(pos == 0, CONV_HALO, 0))
            & (row < jnp.where(pos == chunks_per_seq - 1, CONV_WIN - CONV_HALO, CONV_WIN)))

    for ct in range(c // V7X_LANES):
        lanes = slice(ct * V7X_LANES, (ct + 1) * V7X_LANES)
        win = jnp.where(keep, win_ref[slot, :, lanes], 0.0)
        sh = sh_ref.at[ct % 2]
        for p in range(V7X_SUBLANES):
            sh[p] = win[p:p + CONV_SH_ROWS]
        for r0 in range(0, CONV_TILE, CONV_ROWS):
            acc = jnp.broadcast_to(cb_ref[:, lanes], (CONV_ROWS, V7X_LANES))
            for k in range(CONV_WIDTH):
                off = CONV_TAP0 + k
                rows = pl.ds(r0 + off // V7X_SUBLANES * V7X_SUBLANES, CONV_ROWS)
                acc = acc + sh[off % V7X_SUBLANES, rows, :] * cw_ref[k:k + 1, lanes]
            cv_ref[pl.ds(r0, CONV_ROWS), lanes] = acc
    y = _layer_norm_rows(cv_ref[...], g_ref[...], b_ref[...])
    hc_ref[pl.ds(pl.multiple_of(j * CONV_TILE, CONV_TILE), CONV_TILE), :] = (
        y * jax.nn.sigmoid(y)).astype(hc_ref.dtype)

    gate_ref[...] = jax.nn.sigmoid(_dot(x_ref[...], w_ref[...])).astype(gate_ref.dtype)


def _gates_and_conv(x, w_in, layer, glu, dw_w, dw_b, ln_g, ln_b, seq, side=None):
    t, d = x.shape
    c = glu.shape[1]
    n_gate = N_BRANCH * d
    steps = n_gate // GATE_TN
    assert PROJ_TM == steps * CONV_TILE and seq % CONV_TILE == 0 and OFF_GATE % GATE_TN == 0
    assert CONV_SH_ROWS + V7X_SUBLANES - 1 <= CONV_WIN
    blocks = PROJ_TM * d * 2 + d * GATE_TN * 2 + PROJ_TM * GATE_TN * 2 + PROJ_TM * c * 2
    scratch = (2 * CONV_WIN * c + 2 * V7X_SUBLANES * CONV_SH_ROWS * V7X_LANES + CONV_TILE * c) * 4
    temps = scratch + 3 * PROJ_TM * GATE_TN * 4
    small = lambda r: pl.BlockSpec((r, c), lambda i, j: (0, 0))
    hbm = lambda: pl.BlockSpec(memory_space=pl.ANY)
    side_in, side_out, side_shapes, side_scratch, side_layer = [], [], [], [], None
    if side is not None:
        w_up, w_down, side_layer = side
        up_shape, down_shape = w_up.shape[1:], w_down.shape[1:]
        n_steps = (t // PROJ_TM) * steps
        assert up_shape[0] == n_steps * SIDE_UP_ROWS and 2 <= down_shape[0] // SIDE_DOWN_ROWS <= n_steps
        assert down_shape[0] % SIDE_DOWN_ROWS == 0
        side_in, side_out = [w_up, w_down], [hbm(), hbm()]
        side_shapes = [jax.ShapeDtypeStruct(up_shape, BF16), jax.ShapeDtypeStruct(down_shape, BF16)]
        side_scratch = [pltpu.VMEM((2, SIDE_UP_ROWS, up_shape[1]), F32),
                        pltpu.VMEM((2, SIDE_UP_ROWS, up_shape[1]), BF16),
                        pltpu.VMEM((2, SIDE_DOWN_ROWS, down_shape[1]), F32),
                        pltpu.VMEM((2, SIDE_DOWN_ROWS, down_shape[1]), BF16),
                        pltpu.SemaphoreType.DMA((8,))]
        temps += 2 * (SIDE_UP_ROWS * up_shape[1] + SIDE_DOWN_ROWS * down_shape[1]) * (4 + 2)
    return pl.pallas_call(
        functools.partial(_gate_conv_kernel, seq // CONV_TILE, side_layer),
        grid=(t // PROJ_TM, steps),
        in_specs=[
            pl.BlockSpec((PROJ_TM, d), lambda i, j: (i, 0)),
            pl.BlockSpec((None, d, GATE_TN), lambda i, j: (layer, 0, OFF_GATE // GATE_TN + j)),
            hbm(),
            small(CONV_WIDTH), small(1), small(1), small(1),
        ] + [hbm() for _ in side_in],
        out_specs=[pl.BlockSpec((PROJ_TM, GATE_TN), lambda i, j: (i, j)),
                   pl.BlockSpec((PROJ_TM, c), lambda i, j: (i, 0))] + side_out,
        out_shape=[jax.ShapeDtypeStruct((t, n_gate), BF16), jax.ShapeDtypeStruct((t, c), BF16)] + side_shapes,
        scratch_shapes=[pltpu.VMEM((2, CONV_WIN, c), F32),
                        pltpu.VMEM((2, V7X_SUBLANES, CONV_SH_ROWS, V7X_LANES), F32),
                        pltpu.VMEM((CONV_TILE, c), F32),
                        pltpu.SemaphoreType.DMA((2,))] + side_scratch,
        compiler_params=_params(("arbitrary", "arbitrary"), _vmem_limit(blocks, temps)),
        name="gates_conv",
    )(x, w_in, glu, dw_w, dw_b, ln_g, ln_b, *side_in)


WIN_KEYS = 3 * BLOCK
WIN_BLOCKS = 4


def _masked_softmax_with_sink(sc, sink):
    m = jnp.maximum(jnp.max(sc, axis=1, keepdims=True), sink)
    e = jnp.exp(sc - m)
    denom = jnp.sum(e, axis=1, keepdims=True) + jnp.exp(sink - m)
    return e * (1.0 / denom)


def _win_attn_kernel(sink_ref, q_ref, kv_ref, o_ref):
    s = q_ref.shape[0]
    key_minus_query = (lax.broadcasted_iota(jnp.int32, (BLOCK, WIN_KEYS), 1)
                       - lax.broadcasted_iota(jnp.int32, (BLOCK, WIN_KEYS), 0))
    scale = HEAD_DIM ** -0.5

    def blocks(it, carry):
        chains = []
        for b in range(WIN_BLOCKS):
            q0 = pl.multiple_of((it * WIN_BLOCKS + b) * BLOCK, BLOCK)
            k0 = pl.multiple_of(jnp.clip(q0 - BLOCK, 0, s - WIN_KEYS), BLOCK)
            valid = jnp.abs(key_minus_query + (k0 - q0)) <= WINDOW
            for kvh in range(N_KV_HEADS):
                heads = [kvh * GROUP + g for g in range(GROUP)]
                qg = jnp.concatenate(
                    [q_ref[pl.ds(q0, BLOCK), h * HEAD_DIM:(h + 1) * HEAD_DIM] for h in heads], axis=0)
                k = kv_ref[pl.ds(k0, WIN_KEYS), kvh * HEAD_DIM:(kvh + 1) * HEAD_DIM]
                v = kv_ref[pl.ds(k0, WIN_KEYS),
                           KV_WIDTH + kvh * HEAD_DIM:KV_WIDTH + (kvh + 1) * HEAD_DIM]
                chains.append((q0, heads, valid, _dot_nt(qg, k), v))
        outs = []
        for _, heads, valid, sc_all, v in chains:
            p = [_masked_softmax_with_sink(
                     jnp.where(valid, sc_all[g * BLOCK:(g + 1) * BLOCK] * scale, NEG_INF), sink_ref[h])
                 for g, h in enumerate(heads)]
            outs.append(_dot(jnp.concatenate(p, axis=0).astype(BF16), v))
        for (q0, heads, _, _, _), o in zip(chains, outs):
            for g, h in enumerate(heads):
                o_ref[pl.ds(q0, BLOCK), h * HEAD_DIM:(h + 1) * HEAD_DIM] = (
                    o[g * BLOCK:(g + 1) * BLOCK].astype(o_ref.dtype))
        return carry

    lax.fori_loop(0, s // (BLOCK * WIN_BLOCKS), blocks, 0)


def _win_attention(qkvm, sink, batch, seq):
    x = qkvm.reshape(batch, seq, QKVM_WIDTH)
    kv_block = Q_WIDTH // (2 * KV_WIDTH)
    blocks = seq * (2 * Q_WIDTH + 2 * KV_WIDTH) * 2
    temps = 12 * WIN_BLOCKS * N_KV_HEADS * GROUP * BLOCK * WIN_KEYS * 4
    return pl.pallas_call(
        _win_attn_kernel,
        grid=(batch,),
        in_specs=[
            pl.BlockSpec(memory_space=pltpu.SMEM),
            pl.BlockSpec((None, seq, Q_WIDTH), lambda i: (i, 0, 0)),
            pl.BlockSpec((None, seq, 2 * KV_WIDTH), lambda i: (i, 0, kv_block)),
        ],
        out_specs=pl.BlockSpec((None, seq, Q_WIDTH), lambda i: (i, 0, 0)),
        out_shape=jax.ShapeDtypeStruct((batch, seq, Q_WIDTH), BF16),
        compiler_params=_params(("parallel",), _vmem_limit(blocks, temps)),
        name="win_attn",
    )(sink, x, x)


MEM_ROWS = 512


def _mem_attn_kernel(q_ref, kv_ref, o_ref):
    scale = MEM_HEAD_DIM ** -0.5

    def chunk(i, carry):
        r0 = pl.multiple_of(i * MEM_ROWS, MEM_ROWS)
        cols = [slice(h * MEM_HEAD_DIM, (h + 1) * MEM_HEAD_DIM) for h in range(MEM_HEADS)]
        scores = [_dot_nt(q_ref[pl.ds(r0, MEM_ROWS), c], kv_ref[:, c]) * scale for c in cols]
        outs = []
        for h, sc in enumerate(scores):
            e = jnp.exp(sc - jnp.max(sc, axis=1, keepdims=True))
            p = (e * (1.0 / jnp.sum(e, axis=1, keepdims=True))).astype(BF16)
            vcols = slice(MEM_WIDTH + h * MEM_HEAD_DIM, MEM_WIDTH + (h + 1) * MEM_HEAD_DIM)
            outs.append(_dot(p, kv_ref[:, vcols]))
        for c, o in zip(cols, outs):
            o_ref[pl.ds(r0, MEM_ROWS), c] = o.astype(o_ref.dtype)
        return carry

    lax.fori_loop(0, q_ref.shape[0] // MEM_ROWS, chunk, 0)


def _mem_attention(qkvm, mem_kv, batch, seq):
    x = qkvm.reshape(batch, seq, QKVM_WIDTH)
    kv = mem_kv.reshape(batch, N_MEM, 2 * MEM_WIDTH)
    half = MEM_WIDTH // 2
    first = (Q_WIDTH + 2 * KV_WIDTH) // half
    blocks = seq * 2 * MEM_WIDTH * 2 + N_MEM * 2 * MEM_WIDTH * 2
    temps = 12 * MEM_HEADS * MEM_ROWS * N_MEM * 4

    def body(qa_ref, qb_ref, kv_ref, o_ref, q_ref):
        q_ref[:, :half] = qa_ref[...]
        q_ref[:, half:] = qb_ref[...]
        _mem_attn_kernel(q_ref, kv_ref, o_ref)

    return pl.pallas_call(
        body,
        grid=(batch,),
        in_specs=[
            pl.BlockSpec((None, seq, half), lambda i: (i, 0, first)),
            pl.BlockSpec((None, seq, half), lambda i: (i, 0, first + 1)),
            pl.BlockSpec((None, N_MEM, 2 * MEM_WIDTH), lambda i: (i, 0, 0)),
        ],
        out_specs=pl.BlockSpec((None, seq, MEM_WIDTH), lambda i: (i, 0, 0)),
        out_shape=jax.ShapeDtypeStruct((batch, seq, MEM_WIDTH), BF16),
        scratch_shapes=[pltpu.VMEM((seq, MEM_WIDTH), BF16)],
        compiler_params=_params(("parallel",), _vmem_limit(blocks, temps + seq * MEM_WIDTH * 2)),
        name="mem_attn",
    )(x, x, kv)


MIX_TM = 256


SIDE_UP_ROWS = 32
SIDE_DOWN_ROWS = 176


def _side_cast(src_hbm, dst_hbm, in_ref, out_ref, sems, sem0, layer, step, rows):
    n_chunks = dst_hbm.shape[0] // rows

    def copies(s):
        slot = s % 2
        r = pl.ds(pl.multiple_of(s * rows, rows), rows)
        return (pltpu.make_async_copy(src_hbm.at[layer, r], in_ref.at[slot], sems.at[sem0 + slot]),
                pltpu.make_async_copy(out_ref.at[slot], dst_hbm.at[r], sems.at[sem0 + 2 + slot]))

    @pl.when(step < n_chunks)
    def _():
        fetch, flush = copies(step)

        @pl.when(step == 0)
        def _():
            fetch.start()

        fetch.wait()

        @pl.when(step + 1 < n_chunks)
        def _():
            copies(step + 1)[0].start()

        @pl.when(step >= 2)
        def _():
            copies(step - 2)[1].wait()

        out_ref[step % 2] = in_ref[step % 2].astype(out_ref.dtype)
        flush.start()

        @pl.when(step == n_chunks - 1)
        def _():
            flush.wait()
            copies(step - 1)[1].wait()


def _mix_out_kernel(layer, hc_ref, ow_ref, om_ref, gate_ref, h_ref, wc_ref, ww_ref, wm_ref, wo_ref,
                    g_ref, b_ref, up_hbm, down_hbm, o_ref, ob_ref, upb_hbm, downb_hbm,
                    up_in, up_out, down_in, down_out, sems):
    step = pl.program_id(0)
    _side_cast(up_hbm, upb_hbm, up_in, up_out, sems, 0, layer, step, SIDE_UP_ROWS)
    _side_cast(down_hbm, downb_hbm, down_in, down_out, sems, 4, layer, step, SIDE_DOWN_ROWS)

    d = h_ref.shape[1]
    merged = gate_ref[:, 0:d].astype(F32) * _dot(hc_ref[...], wc_ref[...])
    merged += gate_ref[:, d:2 * d].astype(F32) * _dot(ow_ref[...], ww_ref[...])
    merged += gate_ref[:, 2 * d:3 * d].astype(F32) * _dot(om_ref[...], wm_ref[...])
    y = _dot(merged.astype(BF16), wo_ref[...])
    y = _layer_norm_rows(ALPHA * h_ref[...] + y, g_ref[...], b_ref[...])
    o_ref[...] = y
    ob_ref[...] = y.astype(ob_ref.dtype)


def _mix_out(hc, ow, om, gates, h, w_conv, w_win, w_mem, w_out, layer, ln_g, ln_b, next_up, next_down):
    t, d = h.shape
    steps = t // MIX_TM
    up_shape, down_shape = next_up.shape[1:], next_down.shape[1:]
    assert up_shape[0] == steps * SIDE_UP_ROWS and down_shape[0] % SIDE_DOWN_ROWS == 0
    assert 2 <= down_shape[0] // SIDE_DOWN_ROWS <= steps
    rows = lambda width: pl.BlockSpec((MIX_TM, width), lambda i: (i, 0))
    resident = lambda k: pl.BlockSpec((None, k, d), lambda i: (layer, 0, 0), pipeline_mode=pl.Buffered(1))
    hbm = lambda: pl.BlockSpec(memory_space=pl.ANY)
    weight_bytes = (CONV_CH + Q_WIDTH + MEM_WIDTH + d) * d * 2
    blocks = MIX_TM * ((CONV_CH + Q_WIDTH + MEM_WIDTH) * 2 + N_BRANCH * d * 2 + 2 * d * 4 + d * 2)
    side = 2 * (SIDE_UP_ROWS * up_shape[1] + SIDE_DOWN_ROWS * down_shape[1]) * (4 + 2)
    temps = weight_bytes + 6 * MIX_TM * d * 4 + side
    return pl.pallas_call(
        functools.partial(_mix_out_kernel, layer),
        grid=(steps,),
        in_specs=[
            rows(CONV_CH), rows(Q_WIDTH), rows(MEM_WIDTH), rows(N_BRANCH * d), rows(d),
            resident(CONV_CH), resident(Q_WIDTH), resident(MEM_WIDTH), resident(d),
            pl.BlockSpec((1, d), lambda i: (0, 0)), pl.BlockSpec((1, d), lambda i: (0, 0)),
            hbm(), hbm(),
        ],
        out_specs=[rows(d), rows(d), hbm(), hbm()],
        out_shape=[jax.ShapeDtypeStruct((t, d), F32), jax.ShapeDtypeStruct((t, d), BF16),
                   jax.ShapeDtypeStruct(up_shape, BF16), jax.ShapeDtypeStruct(down_shape, BF16)],
        scratch_shapes=[pltpu.VMEM((2, SIDE_UP_ROWS, up_shape[1]), F32),
                        pltpu.VMEM((2, SIDE_UP_ROWS, up_shape[1]), BF16),
                        pltpu.VMEM((2, SIDE_DOWN_ROWS, down_shape[1]), F32),
                        pltpu.VMEM((2, SIDE_DOWN_ROWS, down_shape[1]), BF16),
                        pltpu.SemaphoreType.DMA((8,))],
        compiler_params=_params(("arbitrary",), _vmem_limit(blocks, temps)),
        name="mix_out",
    )(hc, ow, om, gates, h, w_conv, w_win, w_mem, w_out, ln_g, ln_b, next_up, next_down)


def kernel(x, mem, ln1_g, ln1_b, ffn1_w_up, ffn1_w_down, w_in, conv_dw_w, conv_dw_b, conv_ln_g,
           conv_ln_b, conv_w_out, win_w_o, win_sink, mem_w_kv, mem_w_o, w_out, ln2_g, ln2_b,
           ffn2_w_up, ffn2_w_down, ln3_g, ln3_b):
    batch, seq, d = x.shape
    t = batch * seq
    cos_t, sin_lo, sin_hi = _rope_tables(seq)
    rope_specs = [pl.BlockSpec((PROJ_TM, HEAD_DIM), lambda i, j: (i % (seq // PROJ_TM), 0))] * 3
    row = lambda v: v.reshape(1, -1)
    mem_b = mem.reshape(batch * N_MEM, d).astype(BF16)

    ffn1_up, ffn1_down = _cast_bf16(ffn1_w_up, 1), _cast_bf16(ffn1_w_down, 1)
    w_in_b, mem_kv_b = _cast_bf16(w_in), _cast_bf16(mem_w_kv)
    w_conv_b, w_win_b = _cast_bf16(conv_w_out), _cast_bf16(win_w_o)
    w_mem_b, w_out_b = _cast_bf16(mem_w_o), _cast_bf16(w_out)

    h = x.reshape(t, d)
    hb = _cast_bf16(x.reshape(1, t, d)).reshape(t, d)
    for l in range(DEPTH):
        h, hb = _ffn(hb, h, ffn1_up, ffn1_down, 0, row(ln1_g[l]), row(ln1_b[l]))

        glu = _proj_call(_glu_kernel, hb, w_in_b, l, [OFF_CONV, OFF_CONV + CONV_CH], CONV_CH,
                         [], [], CONV_CH, F32, "proj_glu")
        qkvm = _proj_call(_qkvm_kernel, hb, w_in_b, l, [OFF_QKV], QKVM_TN,
                          [cos_t, sin_lo, sin_hi], rope_specs, QKVM_WIDTH, BF16, "proj_qkvm")
        side = (ffn1_w_up, ffn1_w_down, l + 1) if l + 1 < DEPTH else None
        gates, hc, *next_ffn1 = _gates_and_conv(hb, w_in_b, l, glu, conv_dw_w[l], row(conv_dw_b[l]),
                                                row(conv_ln_g[l]), row(conv_ln_b[l]), seq, side)
        if next_ffn1:
            ffn1_up, ffn1_down = next_ffn1[0][None], next_ffn1[1][None]
        mem_kv = _proj_call(_plain_kernel, mem_b, mem_kv_b, l, [0], MEM_KV_TN,
                            [], [], 2 * MEM_WIDTH, BF16, "proj_mem_kv")

        ow = _win_attention(qkvm, win_sink[l], batch, seq)
        om = _mem_attention(qkvm, mem_kv, batch, seq)

        h, hb, ffn2_up, ffn2_down = _mix_out(
            hc, ow.reshape(t, Q_WIDTH), om.reshape(t, MEM_WIDTH), gates, h, w_conv_b, w_win_b,
            w_mem_b, w_out_b, l, row(ln2_g[l]), row(ln2_b[l]), ffn2_w_up, ffn2_w_down)

        h, hb = _ffn(hb, h, ffn2_up[None], ffn2_down[None], 0, row(ln3_g[l]), row(ln3_b[l]))
    return h.reshape(batch, seq, d)
```
